```python
import math
import jax, jax.numpy as jnp
from jax import lax
import numpy as np

D_MODEL = 1024
BATCH = 8
SEQ = 2048
DEPTH = 4

EPS = 1e-6
HEAD_DIM = 64
ATTN_PATTERNS = ((128, 1), (512, 4), (2048, 16))
N_ATTN_GROUPS = 3
HEADS_PER_GROUP = 8
N_ATTN_HEADS = N_ATTN_GROUPS * HEADS_PER_GROUP
ATTN_WIDTH = N_ATTN_HEADS * HEAD_DIM
ATTN_OUT_WIDTH = HEADS_PER_GROUP * HEAD_DIM
ROPE_THETA = 10000.0
SSD_WIDTH = 2 * D_MODEL
SSD_HEAD_DIM = 64
SSD_HEADS = SSD_WIDTH // SSD_HEAD_DIM
SSD_GROUPS = 8
SSD_HEADS_PER_GROUP = SSD_HEADS // SSD_GROUPS
SSD_STATE = 128
SSD_CHUNK = 128
SSD_CONV = 4
SSD_XBC_WIDTH = SSD_WIDTH + 2 * SSD_GROUPS * SSD_STATE
LRU_BLOCKS = 16
LRU_BLOCK_WIDTH = 80
LRU_WIDTH = LRU_BLOCKS * LRU_BLOCK_WIDTH
LRU_CONV = 4
LRU_C = 8.0
N_BRANCHES = 3
IN_SPLITS = (ATTN_WIDTH, ATTN_WIDTH, ATTN_WIDTH, SSD_WIDTH, SSD_XBC_WIDTH, SSD_HEADS,
             LRU_WIDTH, LRU_WIDTH, N_BRANCHES * D_MODEL)
IN_COLS = sum(IN_SPLITS)
MOE_GROUPS = 4
EXPERTS_PER_GROUP = 8
N_EXPERTS = MOE_GROUPS * EXPERTS_PER_GROUP
TOP_K = 2
EXPERT_FF = 512
MOE_BLOCK = 128

kernel_name = 'hybrid_gated_dilattn_ssd_rglru_hiermoe'


def rmsnorm(x, g):
    xf = x.astype(jnp.float32)
    y = xf * lax.rsqrt(jnp.mean(xf * xf, axis=-1, keepdims=True) + EPS)
    return (y * g.astype(jnp.float32)).astype(x.dtype)


def modulate(h, shift, scale):
    return h * (1 + scale[:, None, :]) + shift[:, None, :]


def rope(t, positions):
    half = t.shape[-1] // 2
    freqs = ROPE_THETA ** (-jnp.arange(half, dtype=jnp.float32) / half)
    ang = positions.astype(jnp.float32)[..., None] * freqs
    cos = jnp.cos(ang)[:, :, None, :]
    sin = jnp.sin(ang)[:, :, None, :]
    t1 = t[..., :half].astype(jnp.float32)
    t2 = t[..., half:].astype(jnp.float32)
    return jnp.concatenate([t1 * cos - t2 * sin, t2 * cos + t1 * sin], axis=-1).astype(t.dtype)


def causal_conv(x, w, b):
    k_w, ch = w.shape
    y = lax.conv_general_dilated(x, w[:, None, :], window_strides=(1,), padding=((k_w - 1, 0),),
                                 dimension_numbers=('NWC', 'WIO', 'NWC'), feature_group_count=ch)
    return y + b


def dilated_window_attention(q, k, v, window, dilation):
    bsz, seq, heads, dh = q.shape
    r = dilation
    band = window // dilation
    sub_len = seq // r
    n_blk = -(-sub_len // band)
    pad_len = n_blk * band - sub_len

    def to_blocks(t):
        t = t.reshape(bsz, sub_len, r, heads, dh).transpose(0, 2, 1, 3, 4)
        t = jnp.pad(t, ((0, 0), (0, 0), (0, pad_len), (0, 0), (0, 0)))
        return t.reshape(bsz, r, n_blk, band, heads, dh)

    def with_prev(t):
        prev = jnp.pad(t, ((0, 0), (0, 0), (1, 0), (0, 0), (0, 0), (0, 0)))[:, :, :-1]
        return jnp.concatenate([prev, t], axis=3)

    qb = to_blocks(q)
    kb = with_prev(to_blocks(k))
    vb = with_prev(to_blocks(v))
    s = jnp.einsum('brnqhd,brnkhd->brnhqk', qb, kb).astype(jnp.float32) * (dh ** -0.5)
    qi = jnp.arange(band)[:, None]
    ki = jnp.arange(2 * band)[None, :]
    dist = qi + band - ki
    key_sub = jnp.arange(n_blk)[:, None, None] * band + ki - band
    valid = (dist >= 0) & (dist <= band) & (key_sub >= 0)
    s = jnp.where(valid[None, None, :, None], s, -jnp.inf)
    m = jnp.max(s, axis=-1, keepdims=True)
    p = jnp.exp(s - m)
    den = jnp.sum(p, axis=-1)
    den_t = jnp.transpose(den, (0, 1, 2, 4, 3))
    o = jnp.einsum('brnhqk,brnkhd->brnqhd', p, vb.astype(jnp.float32)) / den_t[..., None]
    lse = jnp.transpose(m[..., 0], (0, 1, 2, 4, 3)) + jnp.log(den_t)

    def from_blocks(t):
        t = t.reshape((bsz, r, n_blk * band) + t.shape[4:])[:, :, :sub_len]
        t = jnp.moveaxis(t, 1, 2)
        return t.reshape((bsz, seq) + t.shape[3:])

    return from_blocks(o).astype(q.dtype), from_blocks(lse)


def segsum(a):
    t_len = a.shape[-1]
    rep = jnp.broadcast_to(a[..., :, None], a.shape + (t_len,))
    strict = jnp.tril(jnp.ones((t_len, t_len), dtype=bool), -1)
    cs = jnp.cumsum(jnp.where(strict, rep, 0.0), axis=-2)
    return jnp.where(jnp.tril(jnp.ones((t_len, t_len), dtype=bool)), cs, -jnp.inf)


def ssd_chunked(x, a, bm, cm):
    bsz, seq = x.shape[:2]
    nc = seq // SSD_CHUNK
    x = x.reshape(bsz, nc, SSD_CHUNK, SSD_GROUPS, SSD_HEADS_PER_GROUP, SSD_HEAD_DIM)
    bm = bm.reshape(bsz, nc, SSD_CHUNK, SSD_GROUPS, SSD_STATE)
    cm = cm.reshape(bsz, nc, SSD_CHUNK, SSD_GROUPS, SSD_STATE)
    a = a.reshape(bsz, nc, SSD_CHUNK, SSD_GROUPS, SSD_HEADS_PER_GROUP).transpose(0, 3, 4, 1, 2).astype(jnp.float32)
    a_cs = jnp.cumsum(a, axis=-1)
    l_mat = jnp.exp(segsum(a))
    cb = jnp.einsum('bclgn,bcsgn->bcgls', cm, bm)
    y_diag = jnp.einsum('bcgls,bghcls,bcsghp->bclghp', cb, l_mat, x)
    decay_states = jnp.exp(a_cs[..., -1:] - a_cs)
    states = jnp.einsum('bclgn,bghcl,bclghp->bcghpn', bm, decay_states, x)
    states = jnp.concatenate([jnp.zeros_like(states[:, :1]), states], axis=1)
    chunk_decay = jnp.pad(a_cs[..., -1], ((0, 0), (0, 0), (0, 0), (1, 0)))
    dc = jnp.exp(segsum(chunk_decay))
    states = jnp.einsum('bghzc,bcghpn->bzghpn', dc, states)[:, :-1]
    y_off = jnp.einsum('bclgn,bcghpn,bghcl->bclghp', cm, states, jnp.exp(a_cs))
    return (y_diag + y_off).reshape(bsz, seq, SSD_GROUPS, SSD_HEADS_PER_GROUP, SSD_HEAD_DIM)


def ssd_mixer(z, xbc, dt_raw, conv_w, conv_b, dt_bias, a_log, d_skip, norm_g):
    bsz, seq, _ = z.shape
    xbc = jax.nn.silu(causal_conv(xbc, conv_w, conv_b))
    xs, bm, cm = jnp.split(xbc, [SSD_WIDTH, SSD_WIDTH + SSD_GROUPS * SSD_STATE], axis=-1)
    xs = xs.reshape(bsz, seq, SSD_GROUPS, SSD_HEADS_PER_GROUP, SSD_HEAD_DIM)
    bm = bm.reshape(bsz, seq, SSD_GROUPS, SSD_STATE)
    cm = cm.reshape(bsz, seq, SSD_GROUPS, SSD_STATE)
    dt = jax.nn.softplus(dt_raw.astype(jnp.float32) + dt_bias.astype(jnp.float32))
    dt = dt.reshape(bsz, seq, SSD_GROUPS, SSD_HEADS_PER_GROUP)
    a = -jnp.exp(a_log.astype(jnp.float32)).reshape(SSD_GROUPS, SSD_HEADS_PER_GROUP)
    y = ssd_chunked(xs * dt[..., None], dt * a, bm, cm)
    y = y + d_skip.reshape(SSD_GROUPS, SSD_HEADS_PER_GROUP)[..., None] * xs
    y = y.reshape(bsz, seq, SSD_WIDTH).astype(z.dtype)
    return rmsnorm(y * jax.nn.silu(z), norm_g)


def rglru_mixer(gate_in, x_in, conv_w, conv_b, w_r, b_r, w_i, b_i, lam):
    bsz, seq, _ = x_in.shape
    xc = causal_conv(x_in, conv_w, conv_b)
    xb = xc.reshape(bsz, seq, LRU_BLOCKS, LRU_BLOCK_WIDTH)
    r_gate = jax.nn.sigmoid((jnp.einsum('bski,kij->bskj', xb, w_r).reshape(bsz, seq, LRU_WIDTH) + b_r).astype(jnp.float32))
    i_gate = jax.nn.sigmoid((jnp.einsum('bski,kij->bskj', xb, w_i).reshape(bsz, seq, LRU_WIDTH) + b_i).astype(jnp.float32))
    log_a = -LRU_C * r_gate * jax.nn.softplus(-lam.astype(jnp.float32))
    a = jnp.exp(log_a)
    u = jnp.sqrt(-jnp.expm1(2.0 * log_a)) * (i_gate * xc.astype(jnp.float32))

    def combine(left, right):
        a_l, u_l = left
        a_r, u_r = right
        return a_l * a_r, a_r * u_l + u_r

    _, hs = lax.associative_scan(combine, (a, u), axis=1)
    return (hs * jax.nn.gelu(gate_in.astype(jnp.float32))).astype(x_in.dtype)


def mixer_block(h, positions, w_in, ssd_conv_w, ssd_conv_b, ssd_dt_bias, ssd_a_log, ssd_d, ssd_norm_g,
                lru_conv_w, lru_conv_b, lru_w_r, lru_b_r, lru_w_i, lru_b_i, lru_lambda,
                w_br_attn, w_br_ssd, w_br_lru, w_out):
    bsz, seq, _ = h.shape
    proj = h @ w_in
    cuts = np.cumsum(IN_SPLITS)[:-1].tolist()
    q, k, v, z, xbc, dt_raw, lru_gate, lru_x, merge = jnp.split(proj, cuts, axis=-1)
    q = rope(q.reshape(bsz, seq, N_ATTN_HEADS, HEAD_DIM), positions)
    k = rope(k.reshape(bsz, seq, N_ATTN_HEADS, HEAD_DIM), positions)
    q = q.reshape(bsz, seq, N_ATTN_GROUPS, HEADS_PER_GROUP, HEAD_DIM)
    k = k.reshape(bsz, seq, N_ATTN_GROUPS, HEADS_PER_GROUP, HEAD_DIM)
    v = v.reshape(bsz, seq, N_ATTN_GROUPS, HEADS_PER_GROUP, HEAD_DIM)
    outs, lses = [], []
    for gi, (window, dilation) in enumerate(ATTN_PATTERNS):
        o, lse = dilated_window_attention(q[:, :, gi], k[:, :, gi], v[:, :, gi], window, dilation)
        outs.append(o)
        lses.append(lse)
    alpha = jax.nn.softmax(jnp.stack(lses), axis=0)
    y_attn = jnp.einsum('gbsh,gbshd->bshd', alpha, jnp.stack(outs).astype(jnp.float32))
    y_attn = y_attn.reshape(bsz, seq, ATTN_OUT_WIDTH).astype(h.dtype)
    y_ssd = ssd_mixer(z, xbc, dt_raw, ssd_conv_w, ssd_conv_b, ssd_dt_bias, ssd_a_log, ssd_d, ssd_norm_g)
    y_lru = rglru_mixer(lru_gate, lru_x, lru_conv_w, lru_conv_b, lru_w_r, lru_b_r, lru_w_i, lru_b_i, lru_lambda)
    g_attn, g_ssd, g_lru = jnp.split(jax.nn.sigmoid(merge), N_BRANCHES, axis=-1)
    merged = g_attn * (y_attn @ w_br_attn) + g_ssd * (y_ssd @ w_br_ssd) + g_lru * (y_lru @ w_br_lru)
    return merged @ w_out


def hier_moe(h, router_wg, router_bg, router_we, router_be, w_gate, w_up, w_down):
    bsz, seq, dm = h.shape
    n_tok = bsz * seq
    xt = h.reshape(n_tok, dm)
    lg = (xt @ router_wg + router_bg).astype(jnp.float32)
    pg = jax.nn.softmax(lg, axis=-1)
    g_sel = jnp.argmax(lg, axis=-1)
    p_grp = jnp.take_along_axis(pg, g_sel[:, None], axis=-1)
    le = jnp.einsum('td,dge->tge', xt, router_we) + router_be
    le = jnp.take_along_axis(le, g_sel[:, None, None], axis=1)[:, 0].astype(jnp.float32)
    pe = jax.nn.softmax(le, axis=-1)
    top_p, top_i = lax.top_k(pe, TOP_K)
    top_p = top_p / jnp.sum(top_p, axis=-1, keepdims=True)
    weights = p_grp * top_p
    expert = g_sel[:, None] * EXPERTS_PER_GROUP + top_i
    n_asg = n_tok * TOP_K
    flat_e = expert.reshape(n_asg).astype(jnp.int32)
    flat_w = weights.reshape(n_asg)
    flat_tok = jnp.arange(n_asg, dtype=jnp.int32) // TOP_K
    order = jnp.argsort(flat_e)
    se, stok, sw = flat_e[order], flat_tok[order], flat_w[order]
    counts = jnp.bincount(flat_e, length=N_EXPERTS)
    start = jnp.cumsum(counts) - counts
    pcounts = ((counts + MOE_BLOCK - 1) // MOE_BLOCK) * MOE_BLOCK
    pend = jnp.cumsum(pcounts)
    pstart = pend - pcounts
    dest = pstart[se] + jnp.arange(n_asg, dtype=jnp.int32) - start[se]
    n_blocks = -(-n_asg // MOE_BLOCK) + N_EXPERTS
    rows = jnp.full((n_blocks * MOE_BLOCK,), n_tok, dtype=jnp.int32).at[dest].set(stok)
    x_pad = jnp.concatenate([xt, jnp.zeros((1, dm), xt.dtype)], axis=0)
    xd = x_pad[rows].reshape(n_blocks, MOE_BLOCK, dm)
    block_e = jnp.minimum(jnp.searchsorted(pend, jnp.arange(n_blocks) * MOE_BLOCK, side='right'), N_EXPERTS - 1)

    def expert_ffn(args):
        xb, e = args
        return (jax.nn.silu(xb @ w_gate[e]) * (xb @ w_up[e])) @ w_down[e]

    yd = lax.map(expert_ffn, (xd, block_e)).reshape(n_blocks * MOE_BLOCK, dm)
    y = jax.ops.segment_sum(yd[dest] * sw[:, None], stok, num_segments=n_tok)
    return y.reshape(bsz, seq, dm).astype(h.dtype)


def setup_inputs(seed: int = 0) -> dict:
    key = jax.random.key(seed)
    ks = iter(jax.random.split(key, 48))

    def nk():
        return next(ks)

    def dense(shape, fan_in, scale=1.0):
        return (scale * fan_in ** -0.5) * jax.random.normal(nk(), shape, jnp.float32)

    def gain(shape):
        return 1.0 + 0.02 * jax.random.normal(nk(), shape, jnp.float32)

    def small(shape, s=0.02):
        return s * jax.random.normal(nk(), shape, jnp.float32)

    x = jax.random.normal(nk(), (BATCH, SEQ, D_MODEL), jnp.float32)
    c = jax.random.normal(nk(), (BATCH, D_MODEL), jnp.float32)
    offset = jax.random.randint(nk(), (BATCH, 1), 0, 4096, dtype=jnp.int32)
    positions = offset + jnp.arange(SEQ, dtype=jnp.int32)[None, :]
    ada_w = dense((DEPTH, D_MODEL, 6 * D_MODEL), D_MODEL, 0.5)
    ada_b = small((DEPTH, 6 * D_MODEL))
    norm1_g = gain((DEPTH, D_MODEL))
    norm2_g = gain((DEPTH, D_MODEL))
    w_in = dense((DEPTH, D_MODEL, IN_COLS), D_MODEL)
    ssd_conv_w = dense((DEPTH, SSD_CONV, SSD_XBC_WIDTH), SSD_CONV)
    ssd_conv_b = small((DEPTH, SSD_XBC_WIDTH))
    u = jax.random.uniform(nk(), (DEPTH, SSD_HEADS), jnp.float32)
    dt0 = jnp.exp(u * (math.log(0.1) - math.log(0.001)) + math.log(0.001))
    ssd_dt_bias = dt0 + jnp.log(-jnp.expm1(-dt0))
    ssd_a_log = jnp.log(jax.random.uniform(nk(), (DEPTH, SSD_HEADS), jnp.float32, 1.0, 16.0))
    ssd_d = gain((DEPTH, SSD_HEADS))
    ssd_norm_g = gain((DEPTH, SSD_WIDTH))
    lru_conv_w = dense((DEPTH, LRU_CONV, LRU_WIDTH), LRU_CONV)
    lru_conv_b = small((DEPTH, LRU_WIDTH))
    lru_w_r = dense((DEPTH, LRU_BLOCKS, LRU_BLOCK_WIDTH, LRU_BLOCK_WIDTH), LRU_BLOCK_WIDTH)
    lru_b_r = small((DEPTH, LRU_WIDTH))
    lru_w_i = dense((DEPTH, LRU_BLOCKS, LRU_BLOCK_WIDTH, LRU_BLOCK_WIDTH), LRU_BLOCK_WIDTH)
    lru_b_i = small((DEPTH, LRU_WIDTH))
    a0 = jax.random.uniform(nk(), (DEPTH, LRU_WIDTH), jnp.float32, 0.9, 0.999)
    s0 = a0 ** (1.0 / LRU_C)
    lru_lambda = jnp.log(s0) - jnp.log1p(-s0)
    w_br_attn = dense((DEPTH, ATTN_OUT_WIDTH, D_MODEL), ATTN_OUT_WIDTH)
    w_br_ssd = dense((DEPTH, SSD_WIDTH, D_MODEL), SSD_WIDTH)
    w_br_lru = dense((DEPTH, LRU_WIDTH, D_MODEL), LRU_WIDTH)
    w_out = dense((DEPTH, D_MODEL, D_MODEL), D_MODEL)
    router_wg = dense((DEPTH, D_MODEL, MOE_GROUPS), D_MODEL)
    router_bg = small((DEPTH, MOE_GROUPS), 0.01)
    router_we = dense((DEPTH, D_MODEL, MOE_GROUPS, EXPERTS_PER_GROUP), D_MODEL)
    router_be = small((DEPTH, MOE_GROUPS, EXPERTS_PER_GROUP), 0.01)
    exp_w_gate = dense((DEPTH, N_EXPERTS, D_MODEL, EXPERT_FF), D_MODEL)
    exp_w_up = dense((DEPTH, N_EXPERTS, D_MODEL, EXPERT_FF), D_MODEL)
    exp_w_down = dense((DEPTH, N_EXPERTS, EXPERT_FF, D_MODEL), EXPERT_FF)
    final_g = gain((D_MODEL,))
    return {'x': x, 'c': c, 'positions': positions, 'ada_w': ada_w, 'ada_b': ada_b,
            'norm1_g': norm1_g, 'norm2_g': norm2_g, 'w_in': w_in,
            'ssd_conv_w': ssd_conv_w, 'ssd_conv_b': ssd_conv_b, 'ssd_dt_bias': ssd_dt_bias,
            'ssd_a_log': ssd_a_log, 'ssd_d': ssd_d, 'ssd_norm_g': ssd_norm_g,
            'lru_conv_w': lru_conv_w, 'lru_conv_b': lru_conv_b, 'lru_w_r': lru_w_r, 'lru_b_r': lru_b_r,
            'lru_w_i': lru_w_i, 'lru_b_i': lru_b_i, 'lru_lambda': lru_lambda,
            'w_br_attn': w_br_attn, 'w_br_ssd': w_br_ssd, 'w_br_lru': w_br_lru, 'w_out': w_out,
            'router_wg': router_wg, 'router_bg': router_bg, 'router_we': router_we, 'router_be': router_be,
            'exp_w_gate': exp_w_gate, 'exp_w_up': exp_w_up, 'exp_w_down': exp_w_down, 'final_g': final_g}


def reference(x, c, positions, ada_w, ada_b, norm1_g, norm2_g, w_in,
              ssd_conv_w, ssd_conv_b, ssd_dt_bias, ssd_a_log, ssd_d, ssd_norm_g,
              lru_conv_w, lru_conv_b, lru_w_r, lru_b_r, lru_w_i, lru_b_i, lru_lambda,
              w_br_attn, w_br_ssd, w_br_lru, w_out,
              router_wg, router_bg, router_we, router_be,
              exp_w_gate, exp_w_up, exp_w_down, final_g):
    c_act = jax.nn.silu(c)
    for l in range(DEPTH):
        cond = c_act @ ada_w[l] + ada_b[l]
        sh1, sc1, g1, sh2, sc2, g2 = jnp.split(cond, 6, axis=-1)
        h = modulate(rmsnorm(x, norm1_g[l]), sh1, sc1)
        y = mixer_block(h, positions, w_in[l], ssd_conv_w[l], ssd_conv_b[l], ssd_dt_bias[l], ssd_a_log[l],
                        ssd_d[l], ssd_norm_g[l], lru_conv_w[l], lru_conv_b[l], lru_w_r[l], lru_b_r[l],
                        lru_w_i[l], lru_b_i[l], lru_lambda[l], w_br_attn[l], w_br_ssd[l], w_br_lru[l], w_out[l])
        x = x + g1[:, None, :] * y
        h = modulate(rmsnorm(x, norm2_g[l]), sh2, sc2)
        y = hier_moe(h, router_wg[l], router_bg[l], router_we[l], router_be[l],
                     exp_w_gate[l], exp_w_up[l], exp_w_down[l])
        x = x + g2[:, None, :] * y
    return rmsnorm(x, final_g)
```

```python
import functools
import math

import jax
import jax.numpy as jnp
import numpy as np
from jax import lax
from jax.experimental import pallas as pl
from jax.experimental.pallas import tpu as pltpu

F32 = jnp.float32
BF16 = jnp.bfloat16
I32 = jnp.int32

LANES = 128
SUBLANES = 8
VMEM_BYTES = 64 << 20
VMEM_LIMIT = (VMEM_BYTES * 3) // 4

EPS = 1e-6
HEAD_DIM = 64
ATTN_DILATIONS = (1, 4, 16)
ATTN_BLK = 128
HEADS_PER_GROUP = 8
N_ATTN_GROUPS = 3
ATTN_WIDTH = N_ATTN_GROUPS * HEADS_PER_GROUP * HEAD_DIM
ROPE_THETA = 10000.0
SSD_HEAD_DIM = 64
SSD_GROUPS = 8
SSD_HEADS_PER_GROUP = 4
SSD_STATE = 128
SSD_CHUNK = 128
CONV_K = 4
LRU_BLOCKS = 16
LRU_BLOCK_WIDTH = 80
LRU_C = 8.0
MOE_GROUPS = 4
EXPERTS_PER_GROUP = 8
N_EXPERTS = MOE_GROUPS * EXPERTS_PER_GROUP
TOP_K = 2
MOE_BLOCK = 128
NEG = -1e30


def _params(*sem):
    return pltpu.CompilerParams(dimension_semantics=sem, vmem_limit_bytes=VMEM_LIMIT)


def _silu(x):
    return x * jax.nn.sigmoid(x)


def _softplus(x):
    return jnp.maximum(x, 0.0) + jnp.log1p(jnp.exp(-jnp.abs(x)))


def _ada_kernel(c_ref, w_ref, b_ref, o_ref):
    c_act = _silu(c_ref[...]).astype(BF16)
    o_ref[0] = jnp.dot(c_act, w_ref[0].astype(BF16), preferred_element_type=F32) + b_ref[0]


def ada_cond(c, ada_w, ada_b):
    depth, d_model, n = ada_w.shape
    bsz = c.shape[0]
    tn = 2048
    return pl.pallas_call(
        _ada_kernel,
        grid=(depth, n // tn),
        in_specs=[pl.BlockSpec((bsz, d_model), lambda l, j: (0, 0)),
                  pl.BlockSpec((1, d_model, tn), lambda l, j: (l, 0, j)),
                  pl.BlockSpec((1, 1, tn), lambda l, j: (l, 0, j))],
        out_specs=pl.BlockSpec((1, bsz, tn), lambda l, j: (l, 0, j)),
        out_shape=jax.ShapeDtypeStruct((depth, bsz, n), F32),
        compiler_params=_params("parallel", "parallel"),
        name="ada_cond",
    )(c, ada_w, ada_b.reshape(depth, 1, n))


def _norm_kernel(x_ref, g_ref, sc_ref, sh_ref, o_ref):
    x = x_ref[0]
    y = x * lax.rsqrt(jnp.mean(x * x, axis=-1, keepdims=True) + EPS) * g_ref[...]
    o_ref[0] = (y * (1.0 + sc_ref[0]) + sh_ref[0]).astype(o_ref.dtype)


def _plain_norm_kernel(x_ref, g_ref, o_ref):
    x = x_ref[0]
    o_ref[0] = x * lax.rsqrt(jnp.mean(x * x, axis=-1, keepdims=True) + EPS) * g_ref[...]


def norm_modulate(x, g, scale, shift, out_dtype):
    bsz, seq, d = x.shape
    ts = 512
    return pl.pallas_call(
        _norm_kernel,
        grid=(bsz, seq // ts),
        in_specs=[pl.BlockSpec((1, ts, d), lambda b, i: (b, i, 0)),
                  pl.BlockSpec((1, d), lambda b, i: (0, 0)),
                  pl.BlockSpec((1, 1, d), lambda b, i: (b, 0, 0)),
                  pl.BlockSpec((1, 1, d), lambda b, i: (b, 0, 0))],
        out_specs=pl.BlockSpec((1, ts, d), lambda b, i: (b, i, 0)),
        out_shape=jax.ShapeDtypeStruct((bsz, seq, d), out_dtype),
        compiler_params=_params("parallel", "parallel"),
        name="norm_modulate",
    )(x, g.reshape(1, d), scale.reshape(bsz, 1, d), shift.reshape(bsz, 1, d))


def final_norm(x, g):
    bsz, seq, d = x.shape
    ts = 512
    return pl.pallas_call(
        _plain_norm_kernel,
        grid=(bsz, seq // ts),
        in_specs=[pl.BlockSpec((1, ts, d), lambda b, i: (b, i, 0)),
                  pl.BlockSpec((1, d), lambda b, i: (0, 0))],
        out_specs=pl.BlockSpec((1, ts, d), lambda b, i: (b, i, 0)),
        out_shape=jax.ShapeDtypeStruct((bsz, seq, d), F32),
        compiler_params=_params("parallel", "parallel"),
        name="final_norm",
    )(x, g.reshape(1, d))


def _mm_kernel(a_ref, w_ref, o_ref):
    o_ref[...] = jnp.dot(a_ref[...], w_ref[...], preferred_element_type=F32).astype(o_ref.dtype)


def matmul(a, w, tm, tn, out_dtype=F32):
    m, k = a.shape
    n = w.shape[1]
    return pl.pallas_call(
        _mm_kernel,
        grid=(m // tm, n // tn),
        in_specs=[pl.BlockSpec((tm, k), lambda i, j: (i, 0)),
                  pl.BlockSpec((k, tn), lambda i, j: (0, j))],
        out_specs=pl.BlockSpec((tm, tn), lambda i, j: (i, j)),
        out_shape=jax.ShapeDtypeStruct((m, n), out_dtype),
        compiler_params=_params("parallel", "parallel"),
        name="matmul",
    )(a, w)


def _mm_slab_kernel(a_ref, w_ref, o_ref):
    res = jnp.dot(a_ref[0], w_ref[...], preferred_element_type=F32)
    for s in range(o_ref.shape[1]):
        o_ref[0, s] = res[:, s * LANES:(s + 1) * LANES]


def matmul_slabs(a, w, tn):
    bsz, seq, k = a.shape
    n = w.shape[1]
    return pl.pallas_call(
        _mm_slab_kernel,
        grid=(bsz, n // tn),
        in_specs=[pl.BlockSpec((1, seq, k), lambda b, j: (b, 0, 0)),
                  pl.BlockSpec((k, tn), lambda b, j: (0, j))],
        out_specs=pl.BlockSpec((1, tn // LANES, seq, LANES), lambda b, j: (b, j, 0, 0)),
        out_shape=jax.ShapeDtypeStruct((bsz, n // LANES, seq, LANES), F32),
        compiler_params=_params("parallel", "parallel"),
        name="matmul_slabs",
    )(a, w)


def _rope_table_kernel(pos_ref, freq_ref, sign_ref, cos_ref, sin_ref):
    ang = pos_ref[0] * freq_ref[...]
    cos_ref[0] = jnp.cos(ang)
    sin_ref[0] = jnp.sin(ang) * sign_ref[...]


def rope_tables(positions):
    bsz, seq = positions.shape
    half = HEAD_DIM // 2
    freqs = ROPE_THETA ** (-jnp.arange(half, dtype=F32) / half)
    freq_row = jnp.tile(freqs, LANES // half).reshape(1, LANES)
    sign_row = jnp.tile(jnp.concatenate([-jnp.ones((half,), F32), jnp.ones((half,), F32)]), LANES // HEAD_DIM)
    pos = jnp.broadcast_to(positions.astype(F32)[..., None], (bsz, seq, LANES))
    ts = 512
    spec = pl.BlockSpec((1, ts, LANES), lambda b, i: (b, i, 0))
    row = pl.BlockSpec((1, LANES), lambda b, i: (0, 0))
    return pl.pallas_call(
        _rope_table_kernel,
        grid=(bsz, seq // ts),
        in_specs=[spec, row, row],
        out_specs=[spec, spec],
        out_shape=[jax.ShapeDtypeStruct((bsz, seq, LANES), F32)] * 2,
        compiler_params=_params("parallel", "parallel"),
        name="rope_tables",
    )(pos, freq_row, sign_row.reshape(1, LANES))


def _attn_kernel(q1, k1, v1, q2, k2, v2, q3, k3, v3, cos_ref, sin_ref, o_ref, qr, kr, og, lg):
    seq = o_ref.shape[1]
    blk = ATTN_BLK
    rope_rows = 256
    lane_r = lax.broadcasted_iota(I32, (rope_rows, LANES), 1)
    first_half = (lane_r % HEAD_DIM) < (HEAD_DIM // 2)
    lane_b = lax.broadcasted_iota(I32, (blk, LANES), 1)
    head0 = lane_b < HEAD_DIM
    qi = lax.broadcasted_iota(I32, (blk, blk), 0)
    ki = lax.broadcasted_iota(I32, (blk, blk), 1)
    bias_prev = jnp.where(ki >= qi, 0.0, NEG).astype(F32)
    bias_cur = jnp.where(ki <= qi, 0.0, NEG).astype(F32)
    nt = (((1,), (1,)), ((), ()))

    for g, (q_ref, k_ref, v_ref) in enumerate(((q1, k1, v1), (q2, k2, v2), (q3, k3, v3))):
        dil = ATTN_DILATIONS[g]
        n_blk = seq // (dil * blk)

        def rope_body(i, carry, q_ref=q_ref, k_ref=k_ref):
            rows = pl.ds(pl.multiple_of(i * rope_rows, rope_rows), rope_rows)
            cos = cos_ref[0, rows, :]
            sin = sin_ref[0, rows, :]
            for src, dst, scale in ((q_ref, qr, HEAD_DIM ** -0.5), (k_ref, kr, None)):
                t = src[0, 0, rows, :]
                partner = jnp.where(first_half, pltpu.roll(t, LANES - HEAD_DIM // 2, 1),
                                    pltpu.roll(t, HEAD_DIM // 2, 1))
                r = t * cos + partner * sin
                dst[rows, :] = r if scale is None else r * scale
            return carry

        lax.fori_loop(0, seq // rope_rows, rope_body, 0)

        def blk_body(idx, carry, dil=dil, n_blk=n_blk, g=g, v_ref=v_ref):
            phase = idx // n_blk
            n = idx % n_blk
            start = phase + dil * blk * n
            prev_start = jnp.where(n > 0, start - dil * blk, start)
            prev_bias = jnp.where(n > 0, 0.0, NEG).astype(F32)
            cur = pl.ds(start, blk, stride=dil) if dil > 1 else pl.ds(start, blk)
            prev = pl.ds(prev_start, blk, stride=dil) if dil > 1 else pl.ds(prev_start, blk)
            q = qr[cur, :]
            k_cur = kr[cur, :].astype(BF16)
            k_prev = kr[prev, :].astype(BF16)
            v_cur = v_ref[0, 0, cur, :].astype(BF16)
            v_prev = v_ref[0, 0, prev, :].astype(BF16)
            outs, lses = [], []
            for h in range(2):
                qh = jnp.where(head0 if h == 0 else ~head0, q, 0.0).astype(BF16)
                s_prev = lax.dot_general(qh, k_prev, nt, preferred_element_type=F32) + (bias_prev + prev_bias)
                s_cur = lax.dot_general(qh, k_cur, nt, preferred_element_type=F32) + bias_cur
                m = jnp.maximum(jnp.max(s_prev, axis=-1, keepdims=True), jnp.max(s_cur, axis=-1, keepdims=True))
                p_prev = jnp.exp(s_prev - m)
                p_cur = jnp.exp(s_cur - m)
                den = jnp.sum(p_prev, axis=-1, keepdims=True) + jnp.sum(p_cur, axis=-1, keepdims=True)
                o = (jnp.dot(p_prev.astype(BF16), v_prev, preferred_element_type=F32)
                     + jnp.dot(p_cur.astype(BF16), v_cur, preferred_element_type=F32))
                outs.append(o / den)
                lses.append(jnp.broadcast_to(m + jnp.log(den), (blk, LANES)))
            og[g, cur, :] = jnp.where(head0, outs[0], outs[1])
            lg[g, cur, :] = jnp.where(head0, lses[0], lses[1])
            return carry

        lax.fori_loop(0, dil * n_blk, blk_body, 0)

    def mix_body(i, carry):
        rows = pl.ds(pl.multiple_of(i * rope_rows, rope_rows), rope_rows)
        l0, l1, l2 = lg[0, rows, :], lg[1, rows, :], lg[2, rows, :]
        m = jnp.maximum(jnp.maximum(l0, l1), l2)
        e0, e1, e2 = jnp.exp(l0 - m), jnp.exp(l1 - m), jnp.exp(l2 - m)
        tot = e0 + e1 + e2
        y = (e0 / tot) * og[0, rows, :] + (e1 / tot) * og[1, rows, :] + (e2 / tot) * og[2, rows, :]
        o_ref[0, rows, :] = y.astype(o_ref.dtype)
        return carry

    lax.fori_loop(0, seq // rope_rows, mix_body, 0)


def dilated_attention(qkv, cos, sin):
    bsz, n_slabs, seq, _ = qkv.shape
    pairs = HEADS_PER_GROUP * HEAD_DIM // LANES
    per_proj = n_slabs // 3

    def slab(proj, g):
        return pl.BlockSpec((1, 1, seq, LANES), lambda b, j: (b, proj * per_proj + g * pairs + j, 0, 0))

    in_specs = []
    for g in range(N_ATTN_GROUPS):
        in_specs += [slab(0, g), slab(1, g), slab(2, g)]
    table = pl.BlockSpec((1, seq, LANES), lambda b, j: (b, 0, 0))
    in_specs += [table, table]
    return pl.pallas_call(
        _attn_kernel,
        grid=(bsz, pairs),
        in_specs=in_specs,
        out_specs=pl.BlockSpec((1, seq, LANES), lambda b, j: (b, 0, j)),
        out_shape=jax.ShapeDtypeStruct((bsz, seq, pairs * LANES), BF16),
        scratch_shapes=[pltpu.VMEM((seq, LANES), F32), pltpu.VMEM((seq, LANES), F32),
                        pltpu.VMEM((N_ATTN_GROUPS, seq, LANES), F32), pltpu.VMEM((N_ATTN_GROUPS, seq, LANES), F32)],
        compiler_params=_params("parallel", "parallel"),
        name="dilated_attention",
    )(*([qkv] * 9), cos, sin)


def _causal_conv(buf, x, w_ref, b_ref, rows):
    buf[SUBLANES:SUBLANES + rows, :] = x
    acc = b_ref[...] + w_ref[CONV_K - 1:CONV_K, :] * x
    for k in range(CONV_K - 1):
        off = SUBLANES - (CONV_K - 1) + k
        acc = acc + w_ref[k:k + 1, :] * buf[off:off + rows, :]
    buf[0:SUBLANES, :] = buf[rows:rows + SUBLANES, :]
    return acc


def _by_head(cols, lane, width):
    out = cols[-1]
    for j in range(len(cols) - 2, -1, -1):
        out = jnp.where(lane < (j + 1) * width, cols[j], out)
    return out


def _ssd_kernel(xbc_ref, dt_ref, z_ref, cw_ref, cb_ref, dtb_ref, a_ref, dskip_ref, ng_ref, o_ref,
                buf, state, ybuf):
    chunk = SSD_CHUNK
    width = SSD_GROUPS * SSD_HEADS_PER_GROUP * SSD_HEAD_DIM
    gw = SSD_HEADS_PER_GROUP * SSD_HEAD_DIM

    @pl.when(pl.program_id(1) == 0)
    def _():
        buf[0:SUBLANES, :] = jnp.zeros((SUBLANES, buf.shape[1]), F32)
        state[...] = jnp.zeros(state.shape, F32)

    xbc = _silu(_causal_conv(buf, xbc_ref[0], cw_ref, cb_ref, chunk))
    dt = _softplus(dt_ref[0] + dtb_ref[...])
    a = dt * a_ref[...]
    li = lax.broadcasted_iota(I32, (chunk, chunk), 0)
    si = lax.broadcasted_iota(I32, (chunk, chunk), 1)
    tril = li >= si
    tri = jnp.where(tril, 1.0, 0.0).astype(BF16)
    a_hi = a.astype(BF16)
    r1 = a - a_hi.astype(F32)
    a_mid = r1.astype(BF16)
    a_lo = (r1 - a_mid.astype(F32)).astype(BF16)
    a_cs = (jnp.dot(tri, a_hi, preferred_element_type=F32) + jnp.dot(tri, a_mid, preferred_element_type=F32)
            + jnp.dot(tri, a_lo, preferred_element_type=F32))
    a_cs_t = a_cs.T
    a_last = a_cs[chunk - 1:chunk, :]
    e_acs = jnp.exp(a_cs)
    e_dec = jnp.exp(a_last - a_cs)
    e_last = jnp.exp(a_last)
    lane_g = lax.broadcasted_iota(I32, (chunk, gw), 1)
    lane_1 = lax.broadcasted_iota(I32, (1, gw), 1)

    for g in range(SSD_GROUPS):
        heads = range(g * SSD_HEADS_PER_GROUP, (g + 1) * SSD_HEADS_PER_GROUP)
        xs = xbc[:, g * gw:(g + 1) * gw]
        bm = xbc[:, width + g * SSD_STATE: width + (g + 1) * SSD_STATE]
        cm = xbc[:, width + SSD_GROUPS * SSD_STATE + g * SSD_STATE: width + SSD_GROUPS * SSD_STATE + (g + 1) * SSD_STATE]
        bm_t = bm.T.astype(BF16)
        cm_b = cm.astype(BF16)
        cb = jnp.dot(cm_b, bm_t, preferred_element_type=F32)
        dt_g = _by_head([dt[:, h:h + 1] for h in heads], lane_g, SSD_HEAD_DIM)
        x_dt = xs * dt_g
        x_dt_b = x_dt.astype(BF16)
        diag = []
        for h in heads:
            l_mat = jnp.where(tril, jnp.exp(a_cs[:, h:h + 1] - a_cs_t[h:h + 1, :]), 0.0)
            diag.append(jnp.dot((cb * l_mat).astype(BF16), x_dt_b, preferred_element_type=F32))
        y = _by_head(diag, lane_g, SSD_HEAD_DIM)
        s_prev = state[g]
        y = y + jnp.dot(cm_b, s_prev.astype(BF16), preferred_element_type=F32) * _by_head(
            [e_acs[:, h:h + 1] for h in heads], lane_g, SSD_HEAD_DIM)
        x_dec = (x_dt * _by_head([e_dec[:, h:h + 1] for h in heads], lane_g, SSD_HEAD_DIM)).astype(BF16)
        state[g] = (s_prev * _by_head([e_last[:, h:h + 1] for h in heads], lane_1, SSD_HEAD_DIM)
                    + jnp.dot(bm_t, x_dec, preferred_element_type=F32))
        ybuf[:, g * gw:(g + 1) * gw] = y + dskip_ref[:, g * gw:(g + 1) * gw] * xs

    v = ybuf[...] * _silu(z_ref[0])
    o_ref[0] = (v * lax.rsqrt(jnp.mean(v * v, axis=-1, keepdims=True) + EPS) * ng_ref[...]).astype(o_ref.dtype)


def ssd_mixer(xbc, dt_raw, z, conv_w, conv_b, dt_bias, a_log, d_skip, norm_g):
    bsz, seq, cw = xbc.shape
    width = z.shape[-1]
    heads = dt_bias.shape[0]
    pad = LANES - heads
    dtb = jnp.pad(dt_bias, (0, pad)).reshape(1, LANES)
    a_neg = jnp.pad(-jnp.exp(a_log), (0, pad)).reshape(1, LANES)
    dskip = jnp.repeat(d_skip, SSD_HEAD_DIM).reshape(1, width)
    row = lambda n: pl.BlockSpec((1, n), lambda b, c: (0, 0))
    return pl.pallas_call(
        _ssd_kernel,
        grid=(bsz, seq // SSD_CHUNK),
        in_specs=[pl.BlockSpec((1, SSD_CHUNK, cw), lambda b, c: (b, c, 0)),
                  pl.BlockSpec((1, SSD_CHUNK, LANES), lambda b, c: (b, c, 0)),
                  pl.BlockSpec((1, SSD_CHUNK, width), lambda b, c: (b, c, 0)),
                  pl.BlockSpec((CONV_K, cw), lambda b, c: (0, 0)),
                  row(cw), row(LANES), row(LANES), row(width), row(width)],
        out_specs=pl.BlockSpec((1, SSD_CHUNK, width), lambda b, c: (b, c, 0)),
        out_shape=jax.ShapeDtypeStruct((bsz, seq, width), BF16),
        scratch_shapes=[pltpu.VMEM((SSD_CHUNK + SUBLANES, cw), F32),
                        pltpu.VMEM((SSD_GROUPS, SSD_STATE, SSD_HEADS_PER_GROUP * SSD_HEAD_DIM), F32),
                        pltpu.VMEM((SSD_CHUNK, width), F32)],
        compiler_params=_params("parallel", "arbitrary"),
        name="ssd_mixer",
    )(xbc, dt_raw, z, conv_w, conv_b.reshape(1, cw), dtb, a_neg, dskip, norm_g.reshape(1, width))


def _lru_kernel(x_ref, gate_ref, cw_ref, cb_ref, w_ref, br_ref, bi_ref, lam_ref, o_ref, buf, a_s, u_s, h_s):
    rows = x_ref.shape[1]
    width = x_ref.shape[2]
    half = width // 2

    @pl.when(pl.program_id(1) == 0)
    def _():
        buf[0:SUBLANES, :] = jnp.zeros((SUBLANES, width), F32)
        h_s[...] = jnp.zeros(h_s.shape, F32)

    xc = _causal_conv(buf, x_ref[0], cw_ref, cb_ref, rows)
    sp = _softplus(-lam_ref[...])
    for c in range(2):
        cols = slice(c * half, (c + 1) * half)
        pre = jnp.dot(xc[:, cols].astype(BF16), w_ref[c], preferred_element_type=F32)
        r_gate = jax.nn.sigmoid(pre[:, :half] + br_ref[:, cols])
        i_gate = jax.nn.sigmoid(pre[:, half:] + bi_ref[:, cols])
        log_a = -LRU_C * r_gate * sp[:, cols]
        t = jnp.tanh(log_a)
        one_minus_a2 = -2.0 * t / (1.0 - t)
        a_s[:, cols] = jnp.exp(log_a)
        u_s[:, cols] = jnp.sqrt(one_minus_a2) * (i_gate * xc[:, cols])

    sub = lax.broadcasted_iota(I32, (SUBLANES, width), 0)

    def scan_body(i, h):
        tile = pl.ds(pl.multiple_of(i * SUBLANES, SUBLANES), SUBLANES)
        a = a_s[tile, :]
        u = u_s[tile, :]
        for s in (1, 2, 4):
            keep = sub >= s
            a_sh = jnp.where(keep, pltpu.roll(a, s, 0), 1.0)
            u_sh = jnp.where(keep, pltpu.roll(u, s, 0), 0.0)
            u = a * u_sh + u
            a = a * a_sh
        hs = u + a * h
        u_s[tile, :] = hs
        return jnp.broadcast_to(hs[SUBLANES - 1:SUBLANES, :], (SUBLANES, width))

    h_s[...] = lax.fori_loop(0, rows // SUBLANES, scan_body, h_s[...])
    o_ref[0] = (u_s[...] * jax.nn.gelu(gate_ref[0], approximate=True)).astype(o_ref.dtype)


def rglru_mixer(lru_gx, conv_w, conv_b, w_r, b_r, w_i, b_i, lam):
    bsz, seq, w2 = lru_gx.shape
    width = w2 // 2
    half = width // 2
    per_half = LRU_BLOCKS // 2

    def block_diag(w):
        eye = jnp.eye(per_half, dtype=w.dtype)
        wb = w.reshape(2, per_half, LRU_BLOCK_WIDTH, LRU_BLOCK_WIDTH)
        return jnp.einsum('ckij,kl->ckilj', wb, eye).reshape(2, half, half)

    w_cat = jnp.concatenate([block_diag(w_r), block_diag(w_i)], axis=-1).astype(BF16)
    ts = 256
    row = pl.BlockSpec((1, width), lambda b, i: (0, 0))
    return pl.pallas_call(
        _lru_kernel,
        grid=(bsz, seq // ts),
        in_specs=[pl.BlockSpec((1, ts, width), lambda b, i: (b, i, 1)),
                  pl.BlockSpec((1, ts, width), lambda b, i: (b, i, 0)),
                  pl.BlockSpec((CONV_K, width), lambda b, i: (0, 0)),
                  row,
                  pl.BlockSpec((2, half, width), lambda b, i: (0, 0, 0)),
                  row, row, row],
        out_specs=pl.BlockSpec((1, ts, width), lambda b, i: (b, i, 0)),
        out_shape=jax.ShapeDtypeStruct((bsz, seq, width), BF16),
        scratch_shapes=[pltpu.VMEM((ts + SUBLANES, width), F32), pltpu.VMEM((ts, width), F32),
                        pltpu.VMEM((ts, width), F32), pltpu.VMEM((SUBLANES, width), F32)],
        compiler_params=_params("parallel", "arbitrary"),
        name="rglru_mixer",
    )(lru_gx, lru_gx, conv_w, conv_b.reshape(1, width), w_cat, b_r.reshape(1, width), b_i.reshape(1, width),
      lam.reshape(1, width))


def _merge_kernel(x_ref, ya_ref, ys_ref, yl_ref, mg_ref, g1_ref, wa_ref, ws_ref, wl_ref, wo_ref, o_ref):
    d = x_ref.shape[2]
    gates = jax.nn.sigmoid(mg_ref[0])
    merged = (gates[:, :d] * jnp.dot(ya_ref[0], wa_ref[...], preferred_element_type=F32)
              + gates[:, d:2 * d] * jnp.dot(ys_ref[0], ws_ref[...], preferred_element_type=F32)
              + gates[:, 2 * d:] * jnp.dot(yl_ref[0], wl_ref[...], preferred_element_type=F32))
    y = jnp.dot(merged.astype(BF16), wo_ref[...], preferred_element_type=F32)
    o_ref[0] = x_ref[0] + g1_ref[0] * y


def merge_block(x, y_attn, y_ssd, y_lru, merge, g1, w_a, w_s, w_l, w_o):
    bsz, seq, d = x.shape
    ts = 512
    tok = lambda n: pl.BlockSpec((1, ts, n), lambda b, i: (b, i, 0))
    full = lambda w: pl.BlockSpec(w.shape, lambda b, i: (0, 0))
    return pl.pallas_call(
        _merge_kernel,
        grid=(bsz, seq // ts),
        in_specs=[tok(d), tok(y_attn.shape[-1]), tok(y_ssd.shape[-1]), tok(y_lru.shape[-1]), tok(merge.shape[-1]),
                  pl.BlockSpec((1, 1, d), lambda b, i: (b, 0, 0)),
                  full(w_a), full(w_s), full(w_l), full(w_o)],
        out_specs=tok(d),
        out_shape=jax.ShapeDtypeStruct((bsz, seq, d), F32),
        compiler_params=_params("parallel", "parallel"),
        name="merge_block",
    )(x, y_attn, y_ssd, y_lru, merge, g1.reshape(bsz, 1, d), w_a, w_s, w_l, w_o)


ROUTE_E, ROUTE_W, ROUTE_RANK = 0, 2, 4


def _router_kernel(x_ref, g_ref, sc_ref, sh_ref, wr_ref, br_ref, h_ref, route_ref, count_ref, run):
    ts = x_ref.shape[0]

    @pl.when(pl.program_id(0) == 0)
    def _():
        run[...] = jnp.zeros(run.shape, F32)

    x = x_ref[...]
    y = x * lax.rsqrt(jnp.mean(x * x, axis=-1, keepdims=True) + EPS) * g_ref[...]
    h = y * (1.0 + sc_ref[0]) + sh_ref[0]
    h_ref[...] = h
    h_hi = h.astype(BF16)
    r1 = h - h_hi.astype(F32)
    h_mid = r1.astype(BF16)
    h_lo = (r1 - h_mid.astype(F32)).astype(BF16)
    w_hi = wr_ref[0]
    w_lo = wr_ref[1]
    logits = (jnp.dot(h_hi, w_hi, preferred_element_type=F32) + jnp.dot(h_hi, w_lo, preferred_element_type=F32)
              + jnp.dot(h_mid, w_hi, preferred_element_type=F32) + jnp.dot(h_mid, w_lo, preferred_element_type=F32)
              + jnp.dot(h_lo, w_hi, preferred_element_type=F32)) + br_ref[...]
    lane = lax.broadcasted_iota(I32, (ts, LANES), 1)
    big = jnp.int32(LANES)
    lg = jnp.where(lane < MOE_GROUPS, logits, NEG)
    mg = jnp.max(lg, axis=-1, keepdims=True)
    g_sel = jnp.min(jnp.where(lg == mg, lane, big), axis=-1, keepdims=True)
    p_grp = 1.0 / jnp.sum(jnp.exp(lg - mg), axis=-1, keepdims=True)
    lo = MOE_GROUPS + EXPERTS_PER_GROUP * g_sel
    in_grp = (lane >= lo) & (lane < lo + EXPERTS_PER_GROUP)
    le = jnp.where(in_grp, logits, NEG)
    m1 = jnp.max(le, axis=-1, keepdims=True)
    i1 = jnp.min(jnp.where(le == m1, lane, big), axis=-1, keepdims=True)
    den = jnp.sum(jnp.exp(le - m1), axis=-1, keepdims=True)
    le2 = jnp.where(lane == i1, NEG, le)
    m2 = jnp.max(le2, axis=-1, keepdims=True)
    i2 = jnp.min(jnp.where(le2 == m2, lane, big), axis=-1, keepdims=True)
    p1 = 1.0 / den
    p2 = jnp.exp(m2 - m1) / den
    w1 = p_grp * (p1 / (p1 + p2))
    w2 = p_grp * (p2 / (p1 + p2))
    hot1 = lane == i1
    hot2 = lane == i2
    onehot = jnp.where(hot1 | hot2, 1.0, 0.0)
    ri = lax.broadcasted_iota(I32, (ts, ts), 0)
    ci = lax.broadcasted_iota(I32, (ts, ts), 1)
    strict = jnp.where(ri > ci, 1.0, 0.0).astype(BF16)
    before = jnp.dot(strict, onehot.astype(BF16), preferred_element_type=F32) + run[...]
    rank1 = jnp.sum(jnp.where(hot1, before, 0.0), axis=-1, keepdims=True)
    rank2 = jnp.sum(jnp.where(hot2, before, 0.0), axis=-1, keepdims=True)
    run[...] = run[...] + jnp.sum(onehot, axis=0, keepdims=True)
    count_ref[...] = run[...]
    e1 = (i1 - MOE_GROUPS).astype(F32)
    e2 = (i2 - MOE_GROUPS).astype(F32)
    packed = jnp.where(lane == ROUTE_E, e1, 0.0)
    packed = jnp.where(lane == ROUTE_E + 1, e2, packed)
    packed = jnp.where(lane == ROUTE_W, w1, packed)
    packed = jnp.where(lane == ROUTE_W + 1, w2, packed)
    packed = jnp.where(lane == ROUTE_RANK, rank1, packed)
    packed = jnp.where(lane == ROUTE_RANK + 1, rank2, packed)
    route_ref[...] = packed


def moe_router(x, g, scale, shift, router_wg, router_bg, router_we, router_be):
    bsz, seq, d = x.shape
    n_tok = bsz * seq
    ts = 512
    per_batch = seq // ts
    w_all = jnp.concatenate([router_wg, router_we.reshape(d, N_EXPERTS)], axis=1)
    w_all = jnp.pad(w_all, ((0, 0), (0, LANES - w_all.shape[1])))
    w_hi = w_all.astype(BF16)
    w_lo = (w_all - w_hi.astype(F32)).astype(BF16)
    b_all = jnp.concatenate([router_bg, router_be.reshape(N_EXPERTS)])
    b_all = jnp.pad(b_all, (0, LANES - b_all.shape[0])).reshape(1, LANES)
    return pl.pallas_call(
        _router_kernel,
        grid=(n_tok // ts,),
        in_specs=[pl.BlockSpec((ts, d), lambda i: (i, 0)),
                  pl.BlockSpec((1, d), lambda i: (0, 0)),
                  pl.BlockSpec((1, 1, d), lambda i: (i // per_batch, 0, 0)),
                  pl.BlockSpec((1, 1, d), lambda i: (i // per_batch, 0, 0)),
                  pl.BlockSpec((2, d, LANES), lambda i: (0, 0, 0)),
                  pl.BlockSpec((1, LANES), lambda i: (0, 0))],
        out_specs=[pl.BlockSpec((ts, d), lambda i: (i, 0)),
                   pl.BlockSpec((ts, LANES), lambda i: (i, 0)),
                   pl.BlockSpec((1, LANES), lambda i: (0, 0))],
        out_shape=[jax.ShapeDtypeStruct((n_tok, d), F32),
                   jax.ShapeDtypeStruct((n_tok, LANES), F32),
                   jax.ShapeDtypeStruct((1, LANES), F32)],
        scratch_shapes=[pltpu.VMEM((1, LANES), F32)],
        compiler_params=_params("arbitrary"),
        name="moe_router",
    )(x.reshape(n_tok, d), g.reshape(1, d), scale.reshape(bsz, 1, d), shift.reshape(bsz, 1, d),
      jnp.stack([w_hi, w_lo]), b_all)


DISPATCH_CHUNK = 1024


def _dispatch_kernel(dest_ref, h_ref, zero_ref, xd_ref, sem):
    del zero_ref
    base = pl.program_id(0) * DISPATCH_CHUNK

    def copy(i):
        tok = (base + i) // TOP_K
        return pltpu.make_async_copy(h_ref.at[pl.ds(tok, 1), :], xd_ref.at[pl.ds(dest_ref[0, 0, i], 1), :], sem)

    def start(i, carry):
        copy(i).start()
        return carry

    def wait(i, carry):
        copy(i).wait()
        return carry

    lax.fori_loop(0, DISPATCH_CHUNK, start, 0)
    lax.fori_loop(0, DISPATCH_CHUNK, wait, 0)


def moe_dispatch(h, dest, n_rows):
    n_tok, d = h.shape
    n_asg = dest.shape[0]
    steps = n_asg // DISPATCH_CHUNK
    zeros = jnp.zeros((n_rows, d), h.dtype)
    return pl.pallas_call(
        _dispatch_kernel,
        grid=(steps,),
        in_specs=[pl.BlockSpec((1, 1, DISPATCH_CHUNK), lambda i: (i, 0, 0), memory_space=pltpu.SMEM),
                  pl.BlockSpec(memory_space=pl.ANY),
                  pl.BlockSpec(memory_space=pl.ANY)],
        out_specs=pl.BlockSpec(memory_space=pl.ANY),
        out_shape=jax.ShapeDtypeStruct((n_rows, d), h.dtype),
        input_output_aliases={2: 0},
        scratch_shapes=[pltpu.SemaphoreType.DMA(())],
        compiler_params=pltpu.CompilerParams(dimension_semantics=("arbitrary",), has_side_effects=True),
        name="moe_dispatch",
    )(dest.reshape(steps, 1, DISPATCH_CHUNK), h, zeros)


def _expert_kernel(be_ref, nb_ref, xd_ref, wg_ref, wu_ref, wd_ref, o_ref):
    del be_ref

    @pl.when(pl.program_id(0) < nb_ref[0])
    def _():
        xb = xd_ref[...].astype(BF16)
        gate = jnp.dot(xb, wg_ref[0].astype(BF16), preferred_element_type=F32)
        up = jnp.dot(xb, wu_ref[0].astype(BF16), preferred_element_type=F32)
        o_ref[...] = jnp.dot((_silu(gate) * up).astype(BF16), wd_ref[0].astype(BF16), preferred_element_type=F32)

    @pl.when(pl.program_id(0) >= nb_ref[0])
    def _():
        o_ref[...] = jnp.zeros(o_ref.shape, F32)


def moe_experts(xd, block_e, n_used, w_gate, w_up, w_down):
    n_rows, d = xd.shape
    ff = w_gate.shape[-1]
    n_blocks = n_rows // MOE_BLOCK
    grid_spec = pltpu.PrefetchScalarGridSpec(
        num_scalar_prefetch=2,
        grid=(n_blocks,),
        in_specs=[pl.BlockSpec((MOE_BLOCK, d), lambda i, be, nb: (i, 0)),
                  pl.BlockSpec((1, d, ff), lambda i, be, nb: (be[i], 0, 0)),
                  pl.BlockSpec((1, d, ff), lambda i, be, nb: (be[i], 0, 0)),
                  pl.BlockSpec((1, ff, d), lambda i, be, nb: (be[i], 0, 0))],
        out_specs=pl.BlockSpec((MOE_BLOCK, d), lambda i, be, nb: (i, 0)),
    )
    return pl.pallas_call(
        _expert_kernel,
        grid_spec=grid_spec,
        out_shape=jax.ShapeDtypeStruct((n_rows, d), F32),
        compiler_params=_params("arbitrary"),
        name="moe_experts",
    )(block_e, n_used, xd, w_gate, w_up, w_down)


COMBINE_TOKENS = 256


def _combine_kernel(dest_ref, x_ref, route_ref, g2_ref, yd_ref, o_ref, rows, sem):
    n = COMBINE_TOKENS * TOP_K

    def copy(i):
        return pltpu.make_async_copy(yd_ref.at[pl.ds(dest_ref[0, 0, i], 1), :],
                                     rows.at[i % TOP_K, pl.ds(i // TOP_K, 1), :], sem)

    def start(i, carry):
        copy(i).start()
        return carry

    def wait(i, carry):
        copy(i).wait()
        return carry

    lax.fori_loop(0, n, start, 0)
    lax.fori_loop(0, n, wait, 0)
    route = route_ref[...]
    w1 = route[:, ROUTE_W:ROUTE_W + 1]
    w2 = route[:, ROUTE_W + 1:ROUTE_W + 2]
    y = rows[0] * w1 + rows[1] * w2
    o_ref[...] = x_ref[...] + g2_ref[0] * y


def moe_combine(x, route, g2, yd, dest):
    bsz, seq, d = x.shape
    n_tok = bsz * seq
    tt = COMBINE_TOKENS
    per_batch = seq // tt
    steps = n_tok // tt
    return pl.pallas_call(
        _combine_kernel,
        grid=(steps,),
        in_specs=[pl.BlockSpec((1, 1, tt * TOP_K), lambda i: (i, 0, 0), memory_space=pltpu.SMEM),
                  pl.BlockSpec((tt, d), lambda i: (i, 0)),
                  pl.BlockSpec((tt, LANES), lambda i: (i, 0)),
                  pl.BlockSpec((1, 1, d), lambda i: (i // per_batch, 0, 0)),
                  pl.BlockSpec(memory_space=pl.ANY)],
        out_specs=pl.BlockSpec((tt, d), lambda i: (i, 0)),
        out_shape=jax.ShapeDtypeStruct((n_tok, d), F32),
        scratch_shapes=[pltpu.VMEM((TOP_K, tt, d), F32), pltpu.SemaphoreType.DMA(())],
        compiler_params=_params("arbitrary"),
        name="moe_combine",
    )(dest.reshape(steps, 1, tt * TOP_K), x.reshape(n_tok, d), route, g2.reshape(bsz, 1, d), yd).reshape(bsz, seq, d)


def hier_moe_block(x, g, scale, shift, gate2, router_wg, router_bg, router_we, router_be, w_gate, w_up, w_down):
    bsz, seq, d = x.shape
    n_tok = bsz * seq
    n_asg = n_tok * TOP_K
    n_blocks = -(-n_asg // MOE_BLOCK) + N_EXPERTS
    h, route, counts = moe_router(x, g, scale, shift, router_wg, router_bg, router_we, router_be)
    counts = counts[0, MOE_GROUPS:MOE_GROUPS + N_EXPERTS].astype(I32)
    pcounts = ((counts + MOE_BLOCK - 1) // MOE_BLOCK) * MOE_BLOCK
    pend = jnp.cumsum(pcounts)
    pstart = pend - pcounts
    expert = route[:, ROUTE_E:ROUTE_E + TOP_K].astype(I32)
    rank = route[:, ROUTE_RANK:ROUTE_RANK + TOP_K].astype(I32)
    dest = (pstart[expert] + rank).reshape(n_asg)
    block_e = jnp.minimum(jnp.searchsorted(pend, jnp.arange(n_blocks, dtype=I32) * MOE_BLOCK, side='right'),
                          N_EXPERTS - 1).astype(I32)
    n_used = (pend[-1:] // MOE_BLOCK).astype(I32)
    xd = moe_dispatch(h, dest, n_blocks * MOE_BLOCK)
    yd = moe_experts(xd, block_e, n_used, w_gate, w_up, w_down)
    return moe_combine(x, route, gate2, yd, dest)


def kernel(x, c, positions, ada_w, ada_b, norm1_g, norm2_g, w_in, ssd_conv_w, ssd_conv_b, ssd_dt_bias, ssd_a_log, ssd_d, ssd_norm_g, lru_conv_w, lru_conv_b, lru_w_r, lru_b_r, lru_w_i, lru_b_i, lru_lambda, w_br_attn, w_br_ssd, w_br_lru, w_out, router_wg, router_bg, router_we, router_be, exp_w_gate, exp_w_up, exp_w_down, final_g):
    depth = ada_w.shape[0]
    bsz, seq, d = x.shape
    n_tok = bsz * seq
    ssd_width = ssd_norm_g.shape[-1]
    xbc_width = ssd_conv_w.shape[-1]
    ssd_heads = ssd_dt_bias.shape[-1]
    lru_width = lru_lambda.shape[-1]
    o_z = 3 * ATTN_WIDTH
    o_xbc = o_z + ssd_width
    o_dt = o_xbc + xbc_width
    o_lru = o_dt + ssd_heads
    o_merge = o_lru + 2 * lru_width

    cond = ada_cond(c, ada_w, ada_b)
    cos, sin = rope_tables(positions)
    for l in range(depth):
        sh1, sc1, g1, sh2, sc2, g2 = [cond[l, :, i * d:(i + 1) * d] for i in range(6)]
        w = w_in[l].astype(BF16)
        w_dt = jnp.pad(w[:, o_dt:o_lru], ((0, 0), (0, LANES - ssd_heads)))
        h = norm_modulate(x, norm1_g[l], sc1, sh1, BF16)
        h2d = h.reshape(n_tok, d)
        qkv = matmul_slabs(h, w[:, :o_z], 512)
        z = matmul(h2d, w[:, o_z:o_xbc], 2048, 1024)
        xbc = matmul(h2d, w[:, o_xbc:o_dt], 2048, 1024)
        dt_raw = matmul(h2d, w_dt, 2048, LANES)
        lru_gx = matmul(h2d, w[:, o_lru:o_merge], 2048, 1280)
        merge = matmul(h2d, w[:, o_merge:], 2048, 1024)
        y_attn = dilated_attention(qkv, cos, sin)
        y_ssd = ssd_mixer(xbc.reshape(bsz, seq, -1), dt_raw.reshape(bsz, seq, -1), z.reshape(bsz, seq, -1),
                          ssd_conv_w[l], ssd_conv_b[l], ssd_dt_bias[l], ssd_a_log[l], ssd_d[l], ssd_norm_g[l])
        y_lru = rglru_mixer(lru_gx.reshape(bsz, seq, -1), lru_conv_w[l], lru_conv_b[l], lru_w_r[l], lru_b_r[l],
                            lru_w_i[l], lru_b_i[l], lru_lambda[l])
        x = merge_block(x, y_attn, y_ssd, y_lru, merge.reshape(bsz, seq, -1), g1,
                        w_br_attn[l].astype(BF16), w_br_ssd[l].astype(BF16), w_br_lru[l].astype(BF16),
                        w_out[l].astype(BF16))
        x = hier_moe_block(x, norm2_g[l], sc2, sh2, g2, router_wg[l], router_bg[l], router_we[l], router_be[l],
                           exp_w_gate[l], exp_w_up[l], exp_w_down[l])
    return final_norm(x, final_g)
```

```python
import functools
import math

import jax
import jax.numpy as jnp
import numpy as np
from jax import lax
from jax.experimental import pallas as pl
from jax.experimental.pallas import tpu as pltpu

F32 = jnp.float32
BF16 = jnp.bfloat16
I32 = jnp.int32

LANES = 128
SUBLANES = 8
VMEM_BYTES = 64 << 20
VMEM_LIMIT = (VMEM_BYTES * 3) // 4

EPS = 1e-6
HEAD_DIM = 64
ATTN_DILATIONS = (1, 4, 16)
ATTN_BLK = 128
ATTN_UNROLL = 4
HEADS_PER_GROUP = 8
N_ATTN_GROUPS = 3
ATTN_WIDTH = N_ATTN_GROUPS * HEADS_PER_GROUP * HEAD_DIM
ROPE_THETA = 10000.0
SSD_HEAD_DIM = 64
SSD_GROUPS = 8
SSD_HEADS_PER_GROUP = 4
SSD_STATE = 128
SSD_CHUNK = 128
CONV_K = 4
LRU_BLOCKS = 16
LRU_BLOCK_WIDTH = 80
LRU_C = 8.0
MOE_GROUPS = 4
EXPERTS_PER_GROUP = 8
N_EXPERTS = MOE_GROUPS * EXPERTS_PER_GROUP
TOP_K = 2
MOE_BLOCK = 128
NEG = -1e30


def _params(*sem):
    return pltpu.CompilerParams(dimension_semantics=sem, vmem_limit_bytes=VMEM_LIMIT)


def _silu(x):
    return x * jax.nn.sigmoid(x)


def _softplus(x):
    return jnp.maximum(x, 0.0) + jnp.log1p(jnp.exp(-jnp.abs(x)))


def _ada_kernel(c_ref, w_ref, b_ref, o_ref):
    c_act = _silu(c_ref[...]).astype(BF16)
    o_ref[0] = jnp.dot(c_act, w_ref[0].astype(BF16), preferred_element_type=F32) + b_ref[0]


def ada_cond(c, ada_w, ada_b):
    depth, d_model, n = ada_w.shape
    bsz = c.shape[0]
    tn = 2048
    return pl.pallas_call(
        _ada_kernel,
        grid=(depth, n // tn),
        in_specs=[pl.BlockSpec((bsz, d_model), lambda l, j: (0, 0)),
                  pl.BlockSpec((1, d_model, tn), lambda l, j: (l, 0, j)),
                  pl.BlockSpec((1, 1, tn), lambda l, j: (l, 0, j))],
        out_specs=pl.BlockSpec((1, bsz, tn), lambda l, j: (l, 0, j)),
        out_shape=jax.ShapeDtypeStruct((depth, bsz, n), F32),
        compiler_params=_params("parallel", "parallel"),
        name="ada_cond",
    )(c, ada_w, ada_b.reshape(depth, 1, n))


def _norm_kernel(x_ref, g_ref, sc_ref, sh_ref, o_ref):
    x = x_ref[0]
    y = x * lax.rsqrt(jnp.mean(x * x, axis=-1, keepdims=True) + EPS) * g_ref[...]
    o_ref[0] = (y * (1.0 + sc_ref[0]) + sh_ref[0]).astype(o_ref.dtype)


def _plain_norm_kernel(x_ref, g_ref, o_ref):
    x = x_ref[0]
    o_ref[0] = x * lax.rsqrt(jnp.mean(x * x, axis=-1, keepdims=True) + EPS) * g_ref[...]


def norm_modulate(x, g, scale, shift, out_dtype):
    bsz, seq, d = x.shape
    ts = 512
    return pl.pallas_call(
        _norm_kernel,
        grid=(bsz, seq // ts),
        in_specs=[pl.BlockSpec((1, ts, d), lambda b, i: (b, i, 0)),
                  pl.BlockSpec((1, d), lambda b, i: (0, 0)),
                  pl.BlockSpec((1, 1, d), lambda b, i: (b, 0, 0)),
                  pl.BlockSpec((1, 1, d), lambda b, i: (b, 0, 0))],
        out_specs=pl.BlockSpec((1, ts, d), lambda b, i: (b, i, 0)),
        out_shape=jax.ShapeDtypeStruct((bsz, seq, d), out_dtype),
        compiler_params=_params("parallel", "parallel"),
        name="norm_modulate",
    )(x, g.reshape(1, d), scale.reshape(bsz, 1, d), shift.reshape(bsz, 1, d))


def final_norm(x, g):
    bsz, seq, d = x.shape
    ts = 512
    return pl.pallas_call(
        _plain_norm_kernel,
        grid=(bsz, seq // ts),
        in_specs=[pl.BlockSpec((1, ts, d), lambda b, i: (b, i, 0)),
                  pl.BlockSpec((1, d), lambda b, i: (0, 0))],
        out_specs=pl.BlockSpec((1, ts, d), lambda b, i: (b, i, 0)),
        out_shape=jax.ShapeDtypeStruct((bsz, seq, d), F32),
        compiler_params=_params("parallel", "parallel"),
        name="final_norm",
    )(x, g.reshape(1, d))


def _mm_kernel(a_ref, w_ref, o_ref):
    o_ref[...] = jnp.dot(a_ref[...], w_ref[...], preferred_element_type=F32).astype(o_ref.dtype)


def matmul(a, w, tm, tn, out_dtype=F32):
    m, k = a.shape
    n = w.shape[1]
    return pl.pallas_call(
        _mm_kernel,
        grid=(m // tm, n // tn),
        in_specs=[pl.BlockSpec((tm, k), lambda i, j: (i, 0)),
                  pl.BlockSpec((k, tn), lambda i, j: (0, j))],
        out_specs=pl.BlockSpec((tm, tn), lambda i, j: (i, j)),
        out_shape=jax.ShapeDtypeStruct((m, n), out_dtype),
        compiler_params=_params("parallel", "parallel"),
        name="matmul",
    )(a, w)


def _mm_slab_kernel(a_ref, w_ref, o_ref):
    res = jnp.dot(a_ref[0], w_ref[...], preferred_element_type=F32)
    for s in range(o_ref.shape[1]):
        o_ref[0, s] = res[:, s * LANES:(s + 1) * LANES]


def matmul_slabs(a, w, tn):
    bsz, seq, k = a.shape
    n = w.shape[1]
    return pl.pallas_call(
        _mm_slab_kernel,
        grid=(bsz, n // tn),
        in_specs=[pl.BlockSpec((1, seq, k), lambda b, j: (b, 0, 0)),
                  pl.BlockSpec((k, tn), lambda b, j: (0, j))],
        out_specs=pl.BlockSpec((1, tn // LANES, seq, LANES), lambda b, j: (b, j, 0, 0)),
        out_shape=jax.ShapeDtypeStruct((bsz, n // LANES, seq, LANES), F32),
        compiler_params=_params("parallel", "parallel"),
        name="matmul_slabs",
    )(a, w)


def _rope_table_kernel(pos_ref, freq_ref, sign_ref, cos_ref, sin_ref):
    ang = pos_ref[0] * freq_ref[...]
    cos_ref[0] = jnp.cos(ang)
    sin_ref[0] = jnp.sin(ang) * sign_ref[...]


def rope_tables(positions):
    bsz, seq = positions.shape
    half = HEAD_DIM // 2
    freqs = ROPE_THETA ** (-jnp.arange(half, dtype=F32) / half)
    freq_row = jnp.tile(freqs, LANES // half).reshape(1, LANES)
    sign_row = jnp.tile(jnp.concatenate([-jnp.ones((half,), F32), jnp.ones((half,), F32)]), LANES // HEAD_DIM)
    pos = jnp.broadcast_to(positions.astype(F32)[..., None], (bsz, seq, LANES))
    ts = 512
    spec = pl.BlockSpec((1, ts, LANES), lambda b, i: (b, i, 0))
    row = pl.BlockSpec((1, LANES), lambda b, i: (0, 0))
    return pl.pallas_call(
        _rope_table_kernel,
        grid=(bsz, seq // ts),
        in_specs=[spec, row, row],
        out_specs=[spec, spec],
        out_shape=[jax.ShapeDtypeStruct((bsz, seq, LANES), F32)] * 2,
        compiler_params=_params("parallel", "parallel"),
        name="rope_tables",
    )(pos, freq_row, sign_row.reshape(1, LANES))


def _attn_kernel(q1, k1, v1, q2, k2, v2, q3, k3, v3, cos_ref, sin_ref, o_ref, qr, kr, og, lg):
    seq = o_ref.shape[1]
    blk = ATTN_BLK
    rope_rows = 256
    lane_r = lax.broadcasted_iota(I32, (rope_rows, LANES), 1)
    first_half = (lane_r % HEAD_DIM) < (HEAD_DIM // 2)
    lane_b = lax.broadcasted_iota(I32, (blk, LANES), 1)
    head0 = lane_b < HEAD_DIM
    qi = lax.broadcasted_iota(I32, (blk, blk), 0)
    ki = lax.broadcasted_iota(I32, (blk, blk), 1)
    bias_prev = jnp.where(ki >= qi, 0.0, NEG).astype(F32)
    bias_cur = jnp.where(ki <= qi, 0.0, NEG).astype(F32)
    nt = (((1,), (1,)), ((), ()))
    ones = jnp.ones((blk, LANES), BF16)

    for g, (q_ref, k_ref, v_ref) in enumerate(((q1, k1, v1), (q2, k2, v2), (q3, k3, v3))):
        dil = ATTN_DILATIONS[g]
        n_blk = seq // (dil * blk)

        def rope_body(i, carry, q_ref=q_ref, k_ref=k_ref):
            rows = pl.ds(pl.multiple_of(i * rope_rows, rope_rows), rope_rows)
            cos = cos_ref[0, rows, :]
            sin = sin_ref[0, rows, :]
            for src, dst, scale in ((q_ref, qr, HEAD_DIM ** -0.5), (k_ref, kr, None)):
                t = src[0, 0, rows, :]
                partner = jnp.where(first_half, pltpu.roll(t, LANES - HEAD_DIM // 2, 1),
                                    pltpu.roll(t, HEAD_DIM // 2, 1))
                r = t * cos + partner * sin
                dst[rows, :] = r if scale is None else r * scale
            return carry

        lax.fori_loop(0, seq // rope_rows, rope_body, 0)

        def blk_body(it, carry, dil=dil, n_blk=n_blk, g=g, v_ref=v_ref):
            blocks = []
            for u in range(ATTN_UNROLL):
                idx = it * ATTN_UNROLL + u
                phase = idx // n_blk
                n = idx % n_blk
                start = phase + dil * blk * n
                prev_start = jnp.where(n > 0, start - dil * blk, start)
                prev_bias = jnp.where(n > 0, 0.0, NEG).astype(F32)
                cur = pl.ds(start, blk, stride=dil) if dil > 1 else pl.ds(start, blk)
                prev = pl.ds(prev_start, blk, stride=dil) if dil > 1 else pl.ds(prev_start, blk)
                blocks.append((cur, prev, prev_bias))
            scores = []
            for cur, prev, prev_bias in blocks:
                q = qr[cur, :]
                k_cur = kr[cur, :].astype(BF16)
                k_prev = kr[prev, :].astype(BF16)
                for h in range(2):
                    qh = jnp.where(head0 if h == 0 else ~head0, q, 0.0).astype(BF16)
                    s_prev = lax.dot_general(qh, k_prev, nt, preferred_element_type=F32) + (bias_prev + prev_bias)
                    s_cur = lax.dot_general(qh, k_cur, nt, preferred_element_type=F32) + bias_cur
                    scores.append((s_prev, s_cur))
            probs = []
            for s_prev, s_cur in scores:
                m = jnp.max(jnp.maximum(s_prev, s_cur), axis=-1, keepdims=True)
                probs.append((m, jnp.exp(s_prev - m).astype(BF16), jnp.exp(s_cur - m).astype(BF16)))
            for u, (cur, prev, _) in enumerate(blocks):
                v_cur = v_ref[0, 0, cur, :].astype(BF16)
                v_prev = v_ref[0, 0, prev, :].astype(BF16)
                nums, dens, maxs = [], [], []
                for h in range(2):
                    m, p_prev, p_cur = probs[2 * u + h]
                    nums.append(jnp.dot(p_prev, v_prev, preferred_element_type=F32)
                                + jnp.dot(p_cur, v_cur, preferred_element_type=F32))
                    dens.append(jnp.dot(p_prev, ones, preferred_element_type=F32)
                                + jnp.dot(p_cur, ones, preferred_element_type=F32))
                    maxs.append(jnp.broadcast_to(m, (blk, LANES)))
                den = jnp.where(head0, dens[0], dens[1])
                og[g, cur, :] = jnp.where(head0, nums[0], nums[1]) / den
                lg[g, cur, :] = jnp.where(head0, maxs[0], maxs[1]) + jnp.log(den)
            return carry

        lax.fori_loop(0, dil * n_blk // ATTN_UNROLL, blk_body, 0)

    def mix_body(i, carry):
        rows = pl.ds(pl.multiple_of(i * rope_rows, rope_rows), rope_rows)
        l0, l1, l2 = lg[0, rows, :], lg[1, rows, :], lg[2, rows, :]
        m = jnp.maximum(jnp.maximum(l0, l1), l2)
        e0, e1, e2 = jnp.exp(l0 - m), jnp.exp(l1 - m), jnp.exp(l2 - m)
        tot = e0 + e1 + e2
        y = (e0 / tot) * og[0, rows, :] + (e1 / tot) * og[1, rows, :] + (e2 / tot) * og[2, rows, :]
        o_ref[0, rows, :] = y.astype(o_ref.dtype)
        return carry

    lax.fori_loop(0, seq // rope_rows, mix_body, 0)


def dilated_attention(qkv, cos, sin):
    bsz, n_slabs, seq, _ = qkv.shape
    pairs = HEADS_PER_GROUP * HEAD_DIM // LANES
    per_proj = n_slabs // 3

    def slab(proj, g):
        return pl.BlockSpec((1, 1, seq, LANES), lambda b, j: (b, proj * per_proj + g * pairs + j, 0, 0))

    in_specs = []
    for g in range(N_ATTN_GROUPS):
        in_specs += [slab(0, g), slab(1, g), slab(2, g)]
    table = pl.BlockSpec((1, seq, LANES), lambda b, j: (b, 0, 0))
    in_specs += [table, table]
    return pl.pallas_call(
        _attn_kernel,
        grid=(bsz, pairs),
        in_specs=in_specs,
        out_specs=pl.BlockSpec((1, seq, LANES), lambda b, j: (b, 0, j)),
        out_shape=jax.ShapeDtypeStruct((bsz, seq, pairs * LANES), BF16),
        scratch_shapes=[pltpu.VMEM((seq, LANES), F32), pltpu.VMEM((seq, LANES), F32),
                        pltpu.VMEM((N_ATTN_GROUPS, seq, LANES), F32), pltpu.VMEM((N_ATTN_GROUPS, seq, LANES), F32)],
        compiler_params=_params("parallel", "parallel"),
        name="dilated_attention",
    )(*([qkv] * 9), cos, sin)


def _causal_conv(buf, x, w_ref, b_ref, rows):
    buf[SUBLANES:SUBLANES + rows, :] = x
    acc = b_ref[...] + w_ref[CONV_K - 1:CONV_K, :] * x
    for k in range(CONV_K - 1):
        off = SUBLANES - (CONV_K - 1) + k
        acc = acc + w_ref[k:k + 1, :] * buf[off:off + rows, :]
    buf[0:SUBLANES, :] = buf[rows:rows + SUBLANES, :]
    return acc


def _by_head(cols, lane, width):
    out = cols[-1]
    for j in range(len(cols) - 2, -1, -1):
        out = jnp.where(lane < (j + 1) * width, cols[j], out)
    return out


def _ssd_kernel(xbc_ref, dt_ref, z_ref, cw_ref, cb_ref, dtb_ref, a_ref, dskip_ref, ng_ref, o_ref,
                buf, state, ybuf):
    chunk = SSD_CHUNK
    width = SSD_GROUPS * SSD_HEADS_PER_GROUP * SSD_HEAD_DIM
    gw = SSD_HEADS_PER_GROUP * SSD_HEAD_DIM

    @pl.when(pl.program_id(1) == 0)
    def _():
        buf[0:SUBLANES, :] = jnp.zeros((SUBLANES, buf.shape[1]), F32)
        state[...] = jnp.zeros(state.shape, F32)

    xbc = _silu(_causal_conv(buf, xbc_ref[0], cw_ref, cb_ref, chunk))
    dt = _softplus(dt_ref[0] + dtb_ref[...])
    a = dt * a_ref[...]
    li = lax.broadcasted_iota(I32, (chunk, chunk), 0)
    si = lax.broadcasted_iota(I32, (chunk, chunk), 1)
    tril = li >= si
    tri = jnp.where(tril, 1.0, 0.0).astype(BF16)
    a_hi = a.astype(BF16)
    r1 = a - a_hi.astype(F32)
    a_mid = r1.astype(BF16)
    a_lo = (r1 - a_mid.astype(F32)).astype(BF16)
    a_cs = (jnp.dot(tri, a_hi, preferred_element_type=F32) + jnp.dot(tri, a_mid, preferred_element_type=F32)
            + jnp.dot(tri, a_lo, preferred_element_type=F32))
    a_cs_t = a_cs.T
    a_last = a_cs[chunk - 1:chunk, :]
    e_acs = jnp.exp(a_cs)
    e_dec = jnp.exp(a_last - a_cs)
    e_last = jnp.exp(a_last)
    lane_g = lax.broadcasted_iota(I32, (chunk, gw), 1)
    lane_1 = lax.broadcasted_iota(I32, (1, gw), 1)

    for g in range(SSD_GROUPS):
        heads = range(g * SSD_HEADS_PER_GROUP, (g + 1) * SSD_HEADS_PER_GROUP)
        xs = xbc[:, g * gw:(g + 1) * gw]
        bm = xbc[:, width + g * SSD_STATE: width + (g + 1) * SSD_STATE]
        cm = xbc[:, width + SSD_GROUPS * SSD_STATE + g * SSD_STATE: width + SSD_GROUPS * SSD_STATE + (g + 1) * SSD_STATE]
        bm_t = bm.T.astype(BF16)
        cm_b = cm.astype(BF16)
        cb = jnp.dot(cm_b, bm_t, preferred_element_type=F32)
        dt_g = _by_head([dt[:, h:h + 1] for h in heads], lane_g, SSD_HEAD_DIM)
        x_dt = xs * dt_g
        x_dt_b = x_dt.astype(BF16)
        diag = []
        for h in heads:
            l_mat = jnp.where(tril, jnp.exp(a_cs[:, h:h + 1] - a_cs_t[h:h + 1, :]), 0.0)
            diag.append(jnp.dot((cb * l_mat).astype(BF16), x_dt_b, preferred_element_type=F32))
        y = _by_head(diag, lane_g, SSD_HEAD_DIM)
        s_prev = state[g]
        y = y + jnp.dot(cm_b, s_prev.astype(BF16), preferred_element_type=F32) * _by_head(
            [e_acs[:, h:h + 1] for h in heads], lane_g, SSD_HEAD_DIM)
        x_dec = (x_dt * _by_head([e_dec[:, h:h + 1] for h in heads], lane_g, SSD_HEAD_DIM)).astype(BF16)
        state[g] = (s_prev * _by_head([e_last[:, h:h + 1] for h in heads], lane_1, SSD_HEAD_DIM)
                    + jnp.dot(bm_t, x_dec, preferred_element_type=F32))
        ybuf[:, g * gw:(g + 1) * gw] = y + dskip_ref[:, g * gw:(g + 1) * gw] * xs

    v = ybuf[...] * _silu(z_ref[0])
    o_ref[0] = (v * lax.rsqrt(jnp.mean(v * v, axis=-1, keepdims=True) + EPS) * ng_ref[...]).astype(o_ref.dtype)


def ssd_mixer(xbc, dt_raw, z, conv_w, conv_b, dt_bias, a_log, d_skip, norm_g):
    bsz, seq, cw = xbc.shape
    width = z.shape[-1]
    heads = dt_bias.shape[0]
    pad = LANES - heads
    dtb = jnp.pad(dt_bias, (0, pad)).reshape(1, LANES)
    a_neg = jnp.pad(-jnp.exp(a_log), (0, pad)).reshape(1, LANES)
    dskip = jnp.repeat(d_skip, SSD_HEAD_DIM).reshape(1, width)
    row = lambda n: pl.BlockSpec((1, n), lambda b, c: (0, 0))
    return pl.pallas_call(
        _ssd_kernel,
        grid=(bsz, seq // SSD_CHUNK),
        in_specs=[pl.BlockSpec((1, SSD_CHUNK, cw), lambda b, c: (b, c, 0)),
                  pl.BlockSpec((1, SSD_CHUNK, LANES), lambda b, c: (b, c, 0)),
                  pl.BlockSpec((1, SSD_CHUNK, width), lambda b, c: (b, c, 0)),
                  pl.BlockSpec((CONV_K, cw), lambda b, c: (0, 0)),
                  row(cw), row(LANES), row(LANES), row(width), row(width)],
        out_specs=pl.BlockSpec((1, SSD_CHUNK, width), lambda b, c: (b, c, 0)),
        out_shape=jax.ShapeDtypeStruct((bsz, seq, width), BF16),
        scratch_shapes=[pltpu.VMEM((SSD_CHUNK + SUBLANES, cw), F32),
                        pltpu.VMEM((SSD_GROUPS, SSD_STATE, SSD_HEADS_PER_GROUP * SSD_HEAD_DIM), F32),
                        pltpu.VMEM((SSD_CHUNK, width), F32)],
        compiler_params=_params("parallel", "arbitrary"),
        name="ssd_mixer",
    )(xbc, dt_raw, z, conv_w, conv_b.reshape(1, cw), dtb, a_neg, dskip, norm_g.reshape(1, width))


def _lru_kernel(x_ref, gate_ref, cw_ref, cb_ref, w_ref, br_ref, bi_ref, lam_ref, o_ref, buf, a_s, u_s, h_s):
    rows = x_ref.shape[1]
    width = x_ref.shape[2]
    half = width // 2

    @pl.when(pl.program_id(1) == 0)
    def _():
        buf[0:SUBLANES, :] = jnp.zeros((SUBLANES, width), F32)
        h_s[...] = jnp.zeros(h_s.shape, F32)

    xc = _causal_conv(buf, x_ref[0], cw_ref, cb_ref, rows)
    sp = _softplus(-lam_ref[...])
    for c in range(2):
        cols = slice(c * half, (c + 1) * half)
        pre = jnp.dot(xc[:, cols].astype(BF16), w_ref[c], preferred_element_type=F32)
        r_gate = jax.nn.sigmoid(pre[:, :half] + br_ref[:, cols])
        i_gate = jax.nn.sigmoid(pre[:, half:] + bi_ref[:, cols])
        log_a = -LRU_C * r_gate * sp[:, cols]
        t = jnp.tanh(log_a)
        one_minus_a2 = -2.0 * t / (1.0 - t)
        a_s[:, cols] = jnp.exp(log_a)
        u_s[:, cols] = jnp.sqrt(one_minus_a2) * (i_gate * xc[:, cols])

    sub = lax.broadcasted_iota(I32, (SUBLANES, width), 0)

    def scan_body(i, h):
        tile = pl.ds(pl.multiple_of(i * SUBLANES, SUBLANES), SUBLANES)
        a = a_s[tile, :]
        u = u_s[tile, :]
        for s in (1, 2, 4):
            keep = sub >= s
            a_sh = jnp.where(keep, pltpu.roll(a, s, 0), 1.0)
            u_sh = jnp.where(keep, pltpu.roll(u, s, 0), 0.0)
            u = a * u_sh + u
            a = a * a_sh
        hs = u + a * h
        u_s[tile, :] = hs
        return jnp.broadcast_to(hs[SUBLANES - 1:SUBLANES, :], (SUBLANES, width))

    h_s[...] = lax.fori_loop(0, rows // SUBLANES, scan_body, h_s[...])
    o_ref[0] = (u_s[...] * jax.nn.gelu(gate_ref[0], approximate=True)).astype(o_ref.dtype)


def rglru_mixer(lru_gx, conv_w, conv_b, w_r, b_r, w_i, b_i, lam):
    bsz, seq, w2 = lru_gx.shape
    width = w2 // 2
    half = width // 2
    per_half = LRU_BLOCKS // 2

    def block_diag(w):
        eye = jnp.eye(per_half, dtype=w.dtype)
        wb = w.reshape(2, per_half, LRU_BLOCK_WIDTH, LRU_BLOCK_WIDTH)
        return jnp.einsum('ckij,kl->ckilj', wb, eye).reshape(2, half, half)

    w_cat = jnp.concatenate([block_diag(w_r), block_diag(w_i)], axis=-1).astype(BF16)
    ts = 256
    row = pl.BlockSpec((1, width), lambda b, i: (0, 0))
    return pl.pallas_call(
        _lru_kernel,
        grid=(bsz, seq // ts),
        in_specs=[pl.BlockSpec((1, ts, width), lambda b, i: (b, i, 1)),
                  pl.BlockSpec((1, ts, width), lambda b, i: (b, i, 0)),
                  pl.BlockSpec((CONV_K, width), lambda b, i: (0, 0)),
                  row,
                  pl.BlockSpec((2, half, width), lambda b, i: (0, 0, 0)),
                  row, row, row],
        out_specs=pl.BlockSpec((1, ts, width), lambda b, i: (b, i, 0)),
        out_shape=jax.ShapeDtypeStruct((bsz, seq, width), BF16),
        scratch_shapes=[pltpu.VMEM((ts + SUBLANES, width), F32), pltpu.VMEM((ts, width), F32),
                        pltpu.VMEM((ts, width), F32), pltpu.VMEM((SUBLANES, width), F32)],
        compiler_params=_params("parallel", "arbitrary"),
        name="rglru_mixer",
    )(lru_gx, lru_gx, conv_w, conv_b.reshape(1, width), w_cat, b_r.reshape(1, width), b_i.reshape(1, width),
      lam.reshape(1, width))


def _merge_kernel(x_ref, ya_ref, ys_ref, yl_ref, mg_ref, g1_ref, wa_ref, ws_ref, wl_ref, wo_ref, o_ref):
    d = x_ref.shape[2]
    gates = jax.nn.sigmoid(mg_ref[0])
    merged = (gates[:, :d] * jnp.dot(ya_ref[0], wa_ref[...], preferred_element_type=F32)
              + gates[:, d:2 * d] * jnp.dot(ys_ref[0], ws_ref[...], preferred_element_type=F32)
              + gates[:, 2 * d:] * jnp.dot(yl_ref[0], wl_ref[...], preferred_element_type=F32))
    y = jnp.dot(merged.astype(BF16), wo_ref[...], preferred_element_type=F32)
    o_ref[0] = x_ref[0] + g1_ref[0] * y


def merge_block(x, y_attn, y_ssd, y_lru, merge, g1, w_a, w_s, w_l, w_o):
    bsz, seq, d = x.shape
    ts = 512
    tok = lambda n: pl.BlockSpec((1, ts, n), lambda b, i: (b, i, 0))
    full = lambda w: pl.BlockSpec(w.shape, lambda b, i: (0, 0))
    return pl.pallas_call(
        _merge_kernel,
        grid=(bsz, seq // ts),
        in_specs=[tok(d), tok(y_attn.shape[-1]), tok(y_ssd.shape[-1]), tok(y_lru.shape[-1]), tok(merge.shape[-1]),
                  pl.BlockSpec((1, 1, d), lambda b, i: (b, 0, 0)),
                  full(w_a), full(w_s), full(w_l), full(w_o)],
        out_specs=tok(d),
        out_shape=jax.ShapeDtypeStruct((bsz, seq, d), F32),
        compiler_params=_params("parallel", "parallel"),
        name="merge_block",
    )(x, y_attn, y_ssd, y_lru, merge, g1.reshape(bsz, 1, d), w_a, w_s, w_l, w_o)


ROUTE_E, ROUTE_W, ROUTE_RANK = 0, 2, 4


def _router_kernel(x_ref, g_ref, sc_ref, sh_ref, wr_ref, br_ref, h_ref, route_ref, count_ref, run):
    ts = x_ref.shape[0]

    @pl.when(pl.program_id(0) == 0)
    def _():
        run[...] = jnp.zeros(run.shape, F32)

    x = x_ref[...]
    y = x * lax.rsqrt(jnp.mean(x * x, axis=-1, keepdims=True) + EPS) * g_ref[...]
    h = y * (1.0 + sc_ref[0]) + sh_ref[0]
    h_ref[...] = h
    h_hi = h.astype(BF16)
    r1 = h - h_hi.astype(F32)
    h_mid = r1.astype(BF16)
    h_lo = (r1 - h_mid.astype(F32)).astype(BF16)
    w_hi = wr_ref[0]
    w_lo = wr_ref[1]
    logits = (jnp.dot(h_hi, w_hi, preferred_element_type=F32) + jnp.dot(h_hi, w_lo, preferred_element_type=F32)
              + jnp.dot(h_mid, w_hi, preferred_element_type=F32) + jnp.dot(h_mid, w_lo, preferred_element_type=F32)
              + jnp.dot(h_lo, w_hi, preferred_element_type=F32)) + br_ref[...]
    lane = lax.broadcasted_iota(I32, (ts, LANES), 1)
    big = jnp.int32(LANES)
    lg = jnp.where(lane < MOE_GROUPS, logits, NEG)
    mg = jnp.max(lg, axis=-1, keepdims=True)
    g_sel = jnp.min(jnp.where(lg == mg, lane, big), axis=-1, keepdims=True)
    p_grp = 1.0 / jnp.sum(jnp.exp(lg - mg), axis=-1, keepdims=True)
    lo = MOE_GROUPS + EXPERTS_PER_GROUP * g_sel
    in_grp = (lane >= lo) & (lane < lo + EXPERTS_PER_GROUP)
    le = jnp.where(in_grp, logits, NEG)
    m1 = jnp.max(le, axis=-1, keepdims=True)
    i1 = jnp.min(jnp.where(le == m1, lane, big), axis=-1, keepdims=True)
    den = jnp.sum(jnp.exp(le - m1), axis=-1, keepdims=True)
    le2 = jnp.where(lane == i1, NEG, le)
    m2 = jnp.max(le2, axis=-1, keepdims=True)
    i2 = jnp.min(jnp.where(le2 == m2, lane, big), axis=-1, keepdims=True)
    p1 = 1.0 / den
    p2 = jnp.exp(m2 - m1) / den
    w1 = p_grp * (p1 / (p1 + p2))
    w2 = p_grp * (p2 / (p1 + p2))
    hot1 = lane == i1
    hot2 = lane == i2
    onehot = jnp.where(hot1 | hot2, 1.0, 0.0)
    ri = lax.broadcasted_iota(I32, (ts, ts), 0)
    ci = lax.broadcasted_iota(I32, (ts, ts), 1)
    strict = jnp.where(ri > ci, 1.0, 0.0).astype(BF16)
    before = jnp.dot(strict, onehot.astype(BF16), preferred_element_type=F32) + run[...]
    rank1 = jnp.sum(jnp.where(hot1, before, 0.0), axis=-1, keepdims=True)
    rank2 = jnp.sum(jnp.where(hot2, before, 0.0), axis=-1, keepdims=True)
    run[...] = run[...] + jnp.sum(onehot, axis=0, keepdims=True)
    count_ref[...] = run[...]
    e1 = (i1 - MOE_GROUPS).astype(F32)
    e2 = (i2 - MOE_GROUPS).astype(F32)
    packed = jnp.where(lane == ROUTE_E, e1, 0.0)
    packed = jnp.where(lane == ROUTE_E + 1, e2, packed)
    packed = jnp.where(lane == ROUTE_W, w1, packed)
    packed = jnp.where(lane == ROUTE_W + 1, w2, packed)
    packed = jnp.where(lane == ROUTE_RANK, rank1, packed)
    packed = jnp.where(lane == ROUTE_RANK + 1, rank2, packed)
    route_ref[...] = packed


def moe_router(x, g, scale, shift, router_wg, router_bg, router_we, router_be):
    bsz, seq, d = x.shape
    n_tok = bsz * seq
    ts = 512
    per_batch = seq // ts
    w_all = jnp.concatenate([router_wg, router_we.reshape(d, N_EXPERTS)], axis=1)
    w_all = jnp.pad(w_all, ((0, 0), (0, LANES - w_all.shape[1])))
    w_hi = w_all.astype(BF16)
    w_lo = (w_all - w_hi.astype(F32)).astype(BF16)
    b_all = jnp.concatenate([router_bg, router_be.reshape(N_EXPERTS)])
    b_all = jnp.pad(b_all, (0, LANES - b_all.shape[0])).reshape(1, LANES)
    return pl.pallas_call(
        _router_kernel,
        grid=(n_tok // ts,),
        in_specs=[pl.BlockSpec((ts, d), lambda i: (i, 0)),
                  pl.BlockSpec((1, d), lambda i: (0, 0)),
                  pl.BlockSpec((1, 1, d), lambda i: (i // per_batch, 0, 0)),
                  pl.BlockSpec((1, 1, d), lambda i: (i // per_batch, 0, 0)),
                  pl.BlockSpec((2, d, LANES), lambda i: (0, 0, 0)),
                  pl.BlockSpec((1, LANES), lambda i: (0, 0))],
        out_specs=[pl.BlockSpec((ts, d), lambda i: (i, 0)),
                   pl.BlockSpec((ts, LANES), lambda i: (i, 0)),
                   pl.BlockSpec((1, LANES), lambda i: (0, 0))],
        out_shape=[jax.ShapeDtypeStruct((n_tok, d), F32),
                   jax.ShapeDtypeStruct((n_tok, LANES), F32),
                   jax.ShapeDtypeStruct((1, LANES), F32)],
        scratch_shapes=[pltpu.VMEM((1, LANES), F32)],
        compiler_params=_params("arbitrary"),
        name="moe_router",
    )(x.reshape(n_tok, d), g.reshape(1, d), scale.reshape(bsz, 1, d), shift.reshape(bsz, 1, d),
      jnp.stack([w_hi, w_lo]), b_all)


DISPATCH_TOKENS = 512


def _dispatch_kernel(d0_ref, d1_ref, h_ref, zero_ref, xd_ref, sem):
    del zero_ref

    def copies(j):
        src = h_ref.at[pl.ds(j, 1), :]
        return (pltpu.make_async_copy(src, xd_ref.at[pl.ds(d0_ref[0, 0, j], 1), :], sem),
                pltpu.make_async_copy(src, xd_ref.at[pl.ds(d1_ref[0, 0, j], 1), :], sem))

    def start(j, carry):
        for cp in copies(j):
            cp.start()
        return carry

    def wait(j, carry):
        for cp in copies(j):
            cp.wait()
        return carry

    lax.fori_loop(0, DISPATCH_TOKENS, start, 0, unroll=8)
    lax.fori_loop(0, DISPATCH_TOKENS, wait, 0, unroll=8)


def moe_dispatch(h, dest, n_rows):
    n_tok, d = h.shape
    tt = DISPATCH_TOKENS
    steps = n_tok // tt
    zeros = jnp.zeros((n_rows, d), h.dtype)
    idx = pl.BlockSpec((1, 1, tt), lambda i: (i, 0, 0), memory_space=pltpu.SMEM)
    return pl.pallas_call(
        _dispatch_kernel,
        grid=(steps,),
        in_specs=[idx, idx,
                  pl.BlockSpec((tt, d), lambda i: (i, 0)),
                  pl.BlockSpec(memory_space=pl.ANY)],
        out_specs=pl.BlockSpec(memory_space=pl.ANY),
        out_shape=jax.ShapeDtypeStruct((n_rows, d), h.dtype),
        input_output_aliases={3: 0},
        scratch_shapes=[pltpu.SemaphoreType.DMA(())],
        compiler_params=pltpu.CompilerParams(dimension_semantics=("arbitrary",), has_side_effects=True),
        name="moe_dispatch",
    )(dest[:, 0].reshape(steps, 1, tt), dest[:, 1].reshape(steps, 1, tt), h, zeros)


def _expert_kernel(be_ref, nb_ref, xd_ref, wg_ref, wu_ref, wd_ref, o_ref):
    del be_ref

    @pl.when(pl.program_id(0) < nb_ref[0])
    def _():
        xb = xd_ref[...].astype(BF16)
        gate = jnp.dot(xb, wg_ref[0].astype(BF16), preferred_element_type=F32)
        up = jnp.dot(xb, wu_ref[0].astype(BF16), preferred_element_type=F32)
        o_ref[...] = jnp.dot((_silu(gate) * up).astype(BF16), wd_ref[0].astype(BF16), preferred_element_type=F32)

    @pl.when(pl.program_id(0) >= nb_ref[0])
    def _():
        o_ref[...] = jnp.zeros(o_ref.shape, F32)


def moe_experts(xd, block_e, n_used, w_gate, w_up, w_down):
    n_rows, d = xd.shape
    ff = w_gate.shape[-1]
    n_blocks = n_rows // MOE_BLOCK
    grid_spec = pltpu.PrefetchScalarGridSpec(
        num_scalar_prefetch=2,
        grid=(n_blocks,),
        in_specs=[pl.BlockSpec((MOE_BLOCK, d), lambda i, be, nb: (i, 0)),
                  pl.BlockSpec((1, d, ff), lambda i, be, nb: (be[i], 0, 0)),
                  pl.BlockSpec((1, d, ff), lambda i, be, nb: (be[i], 0, 0)),
                  pl.BlockSpec((1, ff, d), lambda i, be, nb: (be[i], 0, 0))],
        out_specs=pl.BlockSpec((MOE_BLOCK, d), lambda i, be, nb: (i, 0)),
    )
    return pl.pallas_call(
        _expert_kernel,
        grid_spec=grid_spec,
        out_shape=jax.ShapeDtypeStruct((n_rows, d), F32),
        compiler_params=_params("arbitrary"),
        name="moe_experts",
    )(block_e, n_used, xd, w_gate, w_up, w_down)


COMBINE_TOKENS = 256


def _combine_kernel(d0_ref, d1_ref, x_ref, route_ref, g2_ref, yd_ref, o_ref, rows, sem):
    def copies(j):
        return (pltpu.make_async_copy(yd_ref.at[pl.ds(d0_ref[0, 0, j], 1), :], rows.at[0, pl.ds(j, 1), :], sem),
                pltpu.make_async_copy(yd_ref.at[pl.ds(d1_ref[0, 0, j], 1), :], rows.at[1, pl.ds(j, 1), :], sem))

    def start(j, carry):
        for cp in copies(j):
            cp.start()
        return carry

    def wait(j, carry):
        for cp in copies(j):
            cp.wait()
        return carry

    lax.fori_loop(0, COMBINE_TOKENS, start, 0, unroll=8)
    lax.fori_loop(0, COMBINE_TOKENS, wait, 0, unroll=8)
    route = route_ref[...]
    w1 = route[:, ROUTE_W:ROUTE_W + 1]
    w2 = route[:, ROUTE_W + 1:ROUTE_W + 2]
    y = rows[0] * w1 + rows[1] * w2
    o_ref[...] = x_ref[...] + g2_ref[0] * y


def moe_combine(x, route, g2, yd, dest):
    bsz, seq, d = x.shape
    n_tok = bsz * seq
    tt = COMBINE_TOKENS
    per_batch = seq // tt
    steps = n_tok // tt
    idx = pl.BlockSpec((1, 1, tt), lambda i: (i, 0, 0), memory_space=pltpu.SMEM)
    return pl.pallas_call(
        _combine_kernel,
        grid=(steps,),
        in_specs=[idx, idx,
                  pl.BlockSpec((tt, d), lambda i: (i, 0)),
                  pl.BlockSpec((tt, LANES), lambda i: (i, 0)),
                  pl.BlockSpec((1, 1, d), lambda i: (i // per_batch, 0, 0)),
                  pl.BlockSpec(memory_space=pl.ANY)],
        out_specs=pl.BlockSpec((tt, d), lambda i: (i, 0)),
        out_shape=jax.ShapeDtypeStruct((n_tok, d), F32),
        scratch_shapes=[pltpu.VMEM((TOP_K, tt, d), F32), pltpu.SemaphoreType.DMA(())],
        compiler_params=_params("arbitrary"),
        name="moe_combine",
    )(dest[:, 0].reshape(steps, 1, tt), dest[:, 1].reshape(steps, 1, tt), x.reshape(n_tok, d), route,
      g2.reshape(bsz, 1, d), yd).reshape(bsz, seq, d)


def hier_moe_block(x, g, scale, shift, gate2, router_wg, router_bg, router_we, router_be, w_gate, w_up, w_down):
    bsz, seq, d = x.shape
    n_tok = bsz * seq
    n_asg = n_tok * TOP_K
    n_blocks = -(-n_asg // MOE_BLOCK) + N_EXPERTS
    h, route, counts = moe_router(x, g, scale, shift, router_wg, router_bg, router_we, router_be)
    counts = counts[0, MOE_GROUPS:MOE_GROUPS + N_EXPERTS].astype(I32)
    pcounts = ((counts + MOE_BLOCK - 1) // MOE_BLOCK) * MOE_BLOCK
    pend = jnp.cumsum(pcounts)
    pstart = pend - pcounts
    expert = route[:, ROUTE_E:ROUTE_E + TOP_K].astype(I32)
    rank = route[:, ROUTE_RANK:ROUTE_RANK + TOP_K].astype(I32)
    experts = jnp.arange(N_EXPERTS, dtype=I32)
    dest = rank + jnp.sum(jnp.where(expert[..., None] == experts, pstart, 0), axis=-1)
    block_rows = jnp.arange(n_blocks, dtype=I32) * MOE_BLOCK
    block_e = jnp.minimum(jnp.sum((pend[None, :] <= block_rows[:, None]).astype(I32), axis=1), N_EXPERTS - 1)
    n_used = (pend[-1:] // MOE_BLOCK).astype(I32)
    xd = moe_dispatch(h, dest, n_blocks * MOE_BLOCK)
    yd = moe_experts(xd, block_e, n_used, w_gate, w_up, w_down)
    return moe_combine(x, route, gate2, yd, dest)


def kernel(x, c, positions, ada_w, ada_b, norm1_g, norm2_g, w_in, ssd_conv_w, ssd_conv_b, ssd_dt_bias, ssd_a_log, ssd_d, ssd_norm_g, lru_conv_w, lru_conv_b, lru_w_r, lru_b_r, lru_w_i, lru_b_i, lru_lambda, w_br_attn, w_br_ssd, w_br_lru, w_out, router_wg, router_bg, router_we, router_be, exp_w_gate, exp_w_up, exp_w_down, final_g):
    depth = ada_w.shape[0]
    bsz, seq, d = x.shape
    n_tok = bsz * seq
    ssd_width = ssd_norm_g.shape[-1]
    xbc_width = ssd_conv_w.shape[-1]
    ssd_heads = ssd_dt_bias.shape[-1]
    lru_width = lru_lambda.shape[-1]
    o_z = 3 * ATTN_WIDTH
    o_xbc = o_z + ssd_width
    o_dt = o_xbc + xbc_width
    o_lru = o_dt + ssd_heads
    o_merge = o_lru + 2 * lru_width

    cond = ada_cond(c, ada_w, ada_b)
    cos, sin = rope_tables(positions)
    for l in range(depth):
        sh1, sc1, g1, sh2, sc2, g2 = [cond[l, :, i * d:(i + 1) * d] for i in range(6)]
        w = w_in[l].astype(BF16)
        w_dt = jnp.pad(w[:, o_dt:o_lru], ((0, 0), (0, LANES - ssd_heads)))
        h = norm_modulate(x, norm1_g[l], sc1, sh1, BF16)
        h2d = h.reshape(n_tok, d)
        qkv = matmul_slabs(h, w[:, :o_z], 512)
        z = matmul(h2d, w[:, o_z:o_xbc], 2048, 1024)
        xbc = matmul(h2d, w[:, o_xbc:o_dt], 2048, 1024)
        dt_raw = matmul(h2d, w_dt, 2048, LANES)
        lru_gx = matmul(h2d, w[:, o_lru:o_merge], 2048, 1280)
        merge = matmul(h2d, w[:, o_merge:], 2048, 1024)
        y_attn = dilated_attention(qkv, cos, sin)
        y_ssd = ssd_mixer(xbc.reshape(bsz, seq, -1), dt_raw.reshape(bsz, seq, -1), z.reshape(bsz, seq, -1),
                          ssd_conv_w[l], ssd_conv_b[l], ssd_dt_bias[l], ssd_a_log[l], ssd_d[l], ssd_norm_g[l])
        y_lru = rglru_mixer(lru_gx.reshape(bsz, seq, -1), lru_conv_w[l], lru_conv_b[l], lru_w_r[l], lru_b_r[l],
                            lru_w_i[l], lru_b_i[l], lru_lambda[l])
        x = merge_block(x, y_attn, y_ssd, y_lru, merge.reshape(bsz, seq, -1), g1,
                        w_br_attn[l].astype(BF16), w_br_ssd[l].astype(BF16), w_br_lru[l].astype(BF16),
                        w_out[l].astype(BF16))
        x = hier_moe_block(x, norm2_g[l], sc2, sh2, g2, router_wg[l], router_bg[l], router_we[l], router_be[l],
                           exp_w_gate[l], exp_w_up[l], exp_w_down[l])
    return final_norm(x, final_g)
```

```python
import functools
import math

import jax
import jax.numpy as jnp
import numpy as np
from jax import lax
from jax.experimental import pallas as pl
from jax.experimental.pallas import tpu as pltpu

F32 = jnp.float32
BF16 = jnp.bfloat16
I32 = jnp.int32

LANES = 128
SUBLANES = 8
VMEM_BYTES = 64 << 20
VMEM_LIMIT = (VMEM_BYTES * 3) // 4

EPS = 1e-6
HEAD_DIM = 64
ATTN_DILATIONS = (1, 4, 16)
ATTN_BLK = 128
ATTN_UNROLL = 4
HEADS_PER_GROUP = 8
N_ATTN_GROUPS = 3
ATTN_WIDTH = N_ATTN_GROUPS * HEADS_PER_GROUP * HEAD_DIM
ROPE_THETA = 10000.0
SSD_HEAD_DIM = 64
SSD_GROUPS = 8
SSD_HEADS_PER_GROUP = 4
SSD_STATE = 128
SSD_CHUNK = 128
CONV_K = 4
LRU_BLOCKS = 16
LRU_BLOCK_WIDTH = 80
LRU_C = 8.0
MOE_GROUPS = 4
EXPERTS_PER_GROUP = 8
N_EXPERTS = MOE_GROUPS * EXPERTS_PER_GROUP
TOP_K = 2
MOE_BLOCK = 128
NEG = -1e30


def _params(*sem):
    return pltpu.CompilerParams(dimension_semantics=sem, vmem_limit_bytes=VMEM_LIMIT)


def _silu(x):
    return x * jax.nn.sigmoid(x)


def _softplus(x):
    return jnp.maximum(x, 0.0) + jnp.log1p(jnp.exp(-jnp.abs(x)))


def _ada_kernel(c_ref, w_ref, b_ref, o_ref):
    c_act = _silu(c_ref[...]).astype(BF16)
    o_ref[0] = jnp.dot(c_act, w_ref[0].astype(BF16), preferred_element_type=F32) + b_ref[0]


def ada_cond(c, ada_w, ada_b):
    depth, d_model, n = ada_w.shape
    bsz = c.shape[0]
    tn = 2048
    return pl.pallas_call(
        _ada_kernel,
        grid=(depth, n // tn),
        in_specs=[pl.BlockSpec((bsz, d_model), lambda l, j: (0, 0)),
                  pl.BlockSpec((1, d_model, tn), lambda l, j: (l, 0, j)),
                  pl.BlockSpec((1, 1, tn), lambda l, j: (l, 0, j))],
        out_specs=pl.BlockSpec((1, bsz, tn), lambda l, j: (l, 0, j)),
        out_shape=jax.ShapeDtypeStruct((depth, bsz, n), F32),
        compiler_params=_params("parallel", "parallel"),
        name="ada_cond",
    )(c, ada_w, ada_b.reshape(depth, 1, n))


def _norm_kernel(x_ref, g_ref, sc_ref, sh_ref, o_ref):
    x = x_ref[0]
    y = x * lax.rsqrt(jnp.mean(x * x, axis=-1, keepdims=True) + EPS) * g_ref[...]
    o_ref[0] = (y * (1.0 + sc_ref[0]) + sh_ref[0]).astype(o_ref.dtype)


def _plain_norm_kernel(x_ref, g_ref, o_ref):
    x = x_ref[0]
    o_ref[0] = x * lax.rsqrt(jnp.mean(x * x, axis=-1, keepdims=True) + EPS) * g_ref[...]


def norm_modulate(x, g, scale, shift, out_dtype):
    bsz, seq, d = x.shape
    ts = 512
    return pl.pallas_call(
        _norm_kernel,
        grid=(bsz, seq // ts),
        in_specs=[pl.BlockSpec((1, ts, d), lambda b, i: (b, i, 0)),
                  pl.BlockSpec((1, d), lambda b, i: (0, 0)),
                  pl.BlockSpec((1, 1, d), lambda b, i: (b, 0, 0)),
                  pl.BlockSpec((1, 1, d), lambda b, i: (b, 0, 0))],
        out_specs=pl.BlockSpec((1, ts, d), lambda b, i: (b, i, 0)),
        out_shape=jax.ShapeDtypeStruct((bsz, seq, d), out_dtype),
        compiler_params=_params("parallel", "parallel"),
        name="norm_modulate",
    )(x, g.reshape(1, d), scale.reshape(bsz, 1, d), shift.reshape(bsz, 1, d))


def final_norm(x, g):
    bsz, seq, d = x.shape
    ts = 512
    return pl.pallas_call(
        _plain_norm_kernel,
        grid=(bsz, seq // ts),
        in_specs=[pl.BlockSpec((1, ts, d), lambda b, i: (b, i, 0)),
                  pl.BlockSpec((1, d), lambda b, i: (0, 0))],
        out_specs=pl.BlockSpec((1, ts, d), lambda b, i: (b, i, 0)),
        out_shape=jax.ShapeDtypeStruct((bsz, seq, d), F32),
        compiler_params=_params("parallel", "parallel"),
        name="final_norm",
    )(x, g.reshape(1, d))


def _mm_kernel(a_ref, w_ref, o_ref):
    o_ref[...] = jnp.dot(a_ref[...], w_ref[0], preferred_element_type=F32).astype(o_ref.dtype)


def matmul(a, w_stack, layer, col0, n, tm, tn, out_dtype=F32):
    m, k = a.shape
    j0 = col0 // tn
    assert col0 % tn == 0 and n % tn == 0 and m % tm == 0
    return pl.pallas_call(
        _mm_kernel,
        grid=(m // tm, n // tn),
        in_specs=[pl.BlockSpec((tm, k), lambda i, j: (i, 0)),
                  pl.BlockSpec((1, k, tn), lambda i, j: (layer, 0, j0 + j))],
        out_specs=pl.BlockSpec((tm, tn), lambda i, j: (i, j)),
        out_shape=jax.ShapeDtypeStruct((m, n), out_dtype),
        compiler_params=_params("parallel", "parallel"),
        name="matmul",
    )(a, w_stack)


def _mm_slab_kernel(a_ref, w_ref, o_ref):
    res = jnp.dot(a_ref[0], w_ref[0], preferred_element_type=F32)
    for s in range(o_ref.shape[1]):
        o_ref[0, s] = res[:, s * LANES:(s + 1) * LANES]


def matmul_slabs(a, w_stack, layer, col0, n, tn):
    bsz, seq, k = a.shape
    j0 = col0 // tn
    assert col0 % tn == 0 and n % tn == 0
    return pl.pallas_call(
        _mm_slab_kernel,
        grid=(bsz, n // tn),
        in_specs=[pl.BlockSpec((1, seq, k), lambda b, j: (b, 0, 0)),
                  pl.BlockSpec((1, k, tn), lambda b, j: (layer, 0, j0 + j))],
        out_specs=pl.BlockSpec((1, tn // LANES, seq, LANES), lambda b, j: (b, j, 0, 0)),
        out_shape=jax.ShapeDtypeStruct((bsz, n // LANES, seq, LANES), F32),
        compiler_params=_params("parallel", "parallel"),
        name="matmul_slabs",
    )(a, w_stack)


def _rope_table_kernel(pos_ref, freq_ref, sign_ref, cos_ref, sin_ref):
    ang = pos_ref[0] * freq_ref[...]
    cos_ref[0] = jnp.cos(ang)
    sin_ref[0] = jnp.sin(ang) * sign_ref[...]


def rope_tables(positions):
    bsz, seq = positions.shape
    half = HEAD_DIM // 2
    freqs = ROPE_THETA ** (-jnp.arange(half, dtype=F32) / half)
    freq_row = jnp.tile(freqs, LANES // half).reshape(1, LANES)
    sign_row = jnp.tile(jnp.concatenate([-jnp.ones((half,), F32), jnp.ones((half,), F32)]), LANES // HEAD_DIM)
    pos = jnp.broadcast_to(positions.astype(F32)[..., None], (bsz, seq, LANES))
    ts = 512
    spec = pl.BlockSpec((1, ts, LANES), lambda b, i: (b, i, 0))
    row = pl.BlockSpec((1, LANES), lambda b, i: (0, 0))
    return pl.pallas_call(
        _rope_table_kernel,
        grid=(bsz, seq // ts),
        in_specs=[spec, row, row],
        out_specs=[spec, spec],
        out_shape=[jax.ShapeDtypeStruct((bsz, seq, LANES), F32)] * 2,
        compiler_params=_params("parallel", "parallel"),
        name="rope_tables",
    )(pos, freq_row, sign_row.reshape(1, LANES))


def _attn_kernel(q1, k1, v1, q2, k2, v2, q3, k3, v3, cos_ref, sin_ref, o_ref, qr, kr, vr, og, lg):
    seq = o_ref.shape[1]
    blk = ATTN_BLK
    pad = kr.shape[0] - seq
    rope_rows = 256
    lane_r = lax.broadcasted_iota(I32, (rope_rows, LANES), 1)
    first_half = (lane_r % HEAD_DIM) < (HEAD_DIM // 2)
    lane_b = lax.broadcasted_iota(I32, (blk, LANES), 1)
    head0 = lane_b < HEAD_DIM
    qi = lax.broadcasted_iota(I32, (2 * blk, 2 * blk), 0) % blk
    ki = lax.broadcasted_iota(I32, (2 * blk, 2 * blk), 1)
    bias_prev = jnp.where(ki >= qi, 0.0, NEG).astype(F32)
    bias_cur = jnp.where(ki - blk <= qi, 0.0, NEG).astype(F32)
    bias_inner = jnp.where(ki < blk, bias_prev, bias_cur)
    bias_first = jnp.where(ki < blk, NEG, bias_cur)
    nt = (((1,), (1,)), ((), ()))
    ones = jnp.ones((2 * blk, LANES), BF16)
    kr[0:pad, :] = jnp.zeros((pad, LANES), F32)
    vr[0:pad, :] = jnp.zeros((pad, LANES), F32)

    for g, (q_ref, k_ref, v_ref) in enumerate(((q1, k1, v1), (q2, k2, v2), (q3, k3, v3))):
        dil = ATTN_DILATIONS[g]
        n_blk = seq // (dil * blk)

        def rope_body(i, carry, q_ref=q_ref, k_ref=k_ref, v_ref=v_ref):
            rows = pl.ds(pl.multiple_of(i * rope_rows, rope_rows), rope_rows)
            rows_p = pl.ds(pl.multiple_of(pad + i * rope_rows, rope_rows), rope_rows)
            cos = cos_ref[0, rows, :]
            sin = sin_ref[0, rows, :]
            for src, scale in ((q_ref, HEAD_DIM ** -0.5), (k_ref, None)):
                t = src[0, 0, rows, :]
                partner = jnp.where(first_half, pltpu.roll(t, LANES - HEAD_DIM // 2, 1),
                                    pltpu.roll(t, HEAD_DIM // 2, 1))
                r = t * cos + partner * sin
                if scale is None:
                    kr[rows_p, :] = r
                else:
                    qr[rows, :] = r * scale
            vr[rows_p, :] = v_ref[0, 0, rows, :]
            return carry

        lax.fori_loop(0, seq // rope_rows, rope_body, 0)

        def blk_body(it, carry, dil=dil, n_blk=n_blk, g=g):
            blocks = []
            for u in range(ATTN_UNROLL):
                idx = it * ATTN_UNROLL + u
                phase = idx // n_blk
                n = idx % n_blk
                start = phase + dil * blk * n
                cur = pl.ds(start, blk, stride=dil) if dil > 1 else pl.ds(start, blk)
                both = (pl.ds(pad + start - dil * blk, 2 * blk, stride=dil) if dil > 1
                        else pl.ds(pad + start - blk, 2 * blk))
                blocks.append((cur, both, n))
            scores = []
            for cur, both, n in blocks:
                q = qr[cur, :]
                q2 = jnp.concatenate([jnp.where(head0, q, 0.0), jnp.where(head0, 0.0, q)], axis=0).astype(BF16)
                s = lax.dot_general(q2, kr[both, :].astype(BF16), nt, preferred_element_type=F32)
                scores.append(s + jnp.where(n > 0, bias_inner, bias_first))
            probs = []
            for s in scores:
                m = jnp.max(jnp.maximum(s[:, :blk], s[:, blk:]), axis=-1, keepdims=True)
                probs.append((m, jnp.exp(s - m).astype(BF16)))
            for (cur, both, _), (m, p) in zip(blocks, probs):
                v1 = jnp.concatenate([vr[both, :].astype(BF16), ones], axis=1)
                r = jnp.dot(p, v1, preferred_element_type=F32)
                den = jnp.where(head0, r[:blk, LANES:], r[blk:, LANES:])
                og[g, cur, :] = jnp.where(head0, r[:blk, :LANES], r[blk:, :LANES]) / den
                m_b = jnp.broadcast_to(m, (2 * blk, LANES))
                lg[g, cur, :] = jnp.where(head0, m_b[:blk], m_b[blk:]) + jnp.log(den)
            return carry

        lax.fori_loop(0, dil * n_blk // ATTN_UNROLL, blk_body, 0)

    def mix_body(i, carry):
        rows = pl.ds(pl.multiple_of(i * rope_rows, rope_rows), rope_rows)
        l0, l1, l2 = lg[0, rows, :], lg[1, rows, :], lg[2, rows, :]
        m = jnp.maximum(jnp.maximum(l0, l1), l2)
        e0, e1, e2 = jnp.exp(l0 - m), jnp.exp(l1 - m), jnp.exp(l2 - m)
        tot = e0 + e1 + e2
        y = (e0 / tot) * og[0, rows, :] + (e1 / tot) * og[1, rows, :] + (e2 / tot) * og[2, rows, :]
        o_ref[0, rows, :] = y.astype(o_ref.dtype)
        return carry

    lax.fori_loop(0, seq // rope_rows, mix_body, 0)


def dilated_attention(qkv, cos, sin):
    bsz, n_slabs, seq, _ = qkv.shape
    pairs = HEADS_PER_GROUP * HEAD_DIM // LANES
    per_proj = n_slabs // 3
    pad = ATTN_BLK * max(ATTN_DILATIONS)

    def slab(proj, g):
        return pl.BlockSpec((1, 1, seq, LANES), lambda b, j: (b, proj * per_proj + g * pairs + j, 0, 0))

    in_specs = []
    for g in range(N_ATTN_GROUPS):
        in_specs += [slab(0, g), slab(1, g), slab(2, g)]
    table = pl.BlockSpec((1, seq, LANES), lambda b, j: (b, 0, 0))
    in_specs += [table, table]
    return pl.pallas_call(
        _attn_kernel,
        grid=(bsz, pairs),
        in_specs=in_specs,
        out_specs=pl.BlockSpec((1, seq, LANES), lambda b, j: (b, 0, j)),
        out_shape=jax.ShapeDtypeStruct((bsz, seq, pairs * LANES), BF16),
        scratch_shapes=[pltpu.VMEM((seq, LANES), F32), pltpu.VMEM((pad + seq, LANES), F32),
                        pltpu.VMEM((pad + seq, LANES), F32),
                        pltpu.VMEM((N_ATTN_GROUPS, seq, LANES), F32), pltpu.VMEM((N_ATTN_GROUPS, seq, LANES), F32)],
        compiler_params=_params("parallel", "parallel"),
        name="dilated_attention",
    )(*([qkv] * 9), cos, sin)


def _causal_conv(buf, x, w_ref, b_ref, rows):
    buf[SUBLANES:SUBLANES + rows, :] = x
    acc = b_ref[...] + w_ref[CONV_K - 1:CONV_K, :] * x
    for k in range(CONV_K - 1):
        off = SUBLANES - (CONV_K - 1) + k
        acc = acc + w_ref[k:k + 1, :] * buf[off:off + rows, :]
    buf[0:SUBLANES, :] = buf[rows:rows + SUBLANES, :]
    return acc


def _by_head(cols, lane, width):
    out = cols[-1]
    for j in range(len(cols) - 2, -1, -1):
        out = jnp.where(lane < (j + 1) * width, cols[j], out)
    return out


def _ssd_kernel(xbc_ref, dt_ref, z_ref, cw_ref, cb_ref, dtb_ref, a_ref, dskip_ref, ng_ref, o_ref,
                buf, state, ybuf):
    chunk = SSD_CHUNK
    width = SSD_GROUPS * SSD_HEADS_PER_GROUP * SSD_HEAD_DIM
    gw = SSD_HEADS_PER_GROUP * SSD_HEAD_DIM

    @pl.when(pl.program_id(1) == 0)
    def _():
        buf[0:SUBLANES, :] = jnp.zeros((SUBLANES, buf.shape[1]), F32)
        state[...] = jnp.zeros(state.shape, F32)

    xbc = _silu(_causal_conv(buf, xbc_ref[0], cw_ref, cb_ref, chunk))
    dt = _softplus(dt_ref[0] + dtb_ref[...])
    a = dt * a_ref[...]
    li = lax.broadcasted_iota(I32, (chunk, chunk), 0)
    si = lax.broadcasted_iota(I32, (chunk, chunk), 1)
    tril = li >= si
    tri = jnp.where(tril, 1.0, 0.0).astype(BF16)
    a_hi = a.astype(BF16)
    r1 = a - a_hi.astype(F32)
    a_mid = r1.astype(BF16)
    a_lo = (r1 - a_mid.astype(F32)).astype(BF16)
    a_cs = (jnp.dot(tri, a_hi, preferred_element_type=F32) + jnp.dot(tri, a_mid, preferred_element_type=F32)
            + jnp.dot(tri, a_lo, preferred_element_type=F32))
    a_cs_t = a_cs.T
    a_last = a_cs[chunk - 1:chunk, :]
    e_acs = jnp.exp(a_cs)
    e_dec = jnp.exp(a_last - a_cs)
    e_last = jnp.exp(a_last)
    lane_g = lax.broadcasted_iota(I32, (chunk, gw), 1)
    lane_1 = lax.broadcasted_iota(I32, (1, gw), 1)

    for g in range(SSD_GROUPS):
        heads = range(g * SSD_HEADS_PER_GROUP, (g + 1) * SSD_HEADS_PER_GROUP)
        xs = xbc[:, g * gw:(g + 1) * gw]
        bm = xbc[:, width + g * SSD_STATE: width + (g + 1) * SSD_STATE]
        cm = xbc[:, width + SSD_GROUPS * SSD_STATE + g * SSD_STATE: width + SSD_GROUPS * SSD_STATE + (g + 1) * SSD_STATE]
        bm_t = bm.T.astype(BF16)
        cm_b = cm.astype(BF16)
        cb = jnp.dot(cm_b, bm_t, preferred_element_type=F32)
        dt_g = _by_head([dt[:, h:h + 1] for h in heads], lane_g, SSD_HEAD_DIM)
        x_dt = xs * dt_g
        x_dt_b = x_dt.astype(BF16)
        diag = []
        for h in heads:
            l_mat = jnp.where(tril, jnp.exp(a_cs[:, h:h + 1] - a_cs_t[h:h + 1, :]), 0.0)
            diag.append(jnp.dot((cb * l_mat).astype(BF16), x_dt_b, preferred_element_type=F32))
        y = _by_head(diag, lane_g, SSD_HEAD_DIM)
        s_prev = state[g]
        y = y + jnp.dot(cm_b, s_prev.astype(BF16), preferred_element_type=F32) * _by_head(
            [e_acs[:, h:h + 1] for h in heads], lane_g, SSD_HEAD_DIM)
        x_dec = (x_dt * _by_head([e_dec[:, h:h + 1] for h in heads], lane_g, SSD_HEAD_DIM)).astype(BF16)
        state[g] = (s_prev * _by_head([e_last[:, h:h + 1] for h in heads], lane_1, SSD_HEAD_DIM)
                    + jnp.dot(bm_t, x_dec, preferred_element_type=F32))
        ybuf[:, g * gw:(g + 1) * gw] = y + dskip_ref[:, g * gw:(g + 1) * gw] * xs

    v = ybuf[...] * _silu(z_ref[0])
    o_ref[0] = (v * lax.rsqrt(jnp.mean(v * v, axis=-1, keepdims=True) + EPS) * ng_ref[...]).astype(o_ref.dtype)


def ssd_mixer(xbc, dt_raw, z, conv_w, conv_b, dt_bias, a_log, d_skip, norm_g):
    bsz, seq, cw = xbc.shape
    width = z.shape[-1]
    heads = dt_bias.shape[0]
    pad = LANES - heads
    dtb = jnp.pad(dt_bias, (0, pad)).reshape(1, LANES)
    a_neg = jnp.pad(-jnp.exp(a_log), (0, pad)).reshape(1, LANES)
    dskip = jnp.repeat(d_skip, SSD_HEAD_DIM).reshape(1, width)
    row = lambda n: pl.BlockSpec((1, n), lambda b, c: (0, 0))
    return pl.pallas_call(
        _ssd_kernel,
        grid=(bsz, seq // SSD_CHUNK),
        in_specs=[pl.BlockSpec((1, SSD_CHUNK, cw), lambda b, c: (b, c, 0)),
                  pl.BlockSpec((1, SSD_CHUNK, LANES), lambda b, c: (b, c, 0)),
                  pl.BlockSpec((1, SSD_CHUNK, width), lambda b, c: (b, c, 0)),
                  pl.BlockSpec((CONV_K, cw), lambda b, c: (0, 0)),
                  row(cw), row(LANES), row(LANES), row(width), row(width)],
        out_specs=pl.BlockSpec((1, SSD_CHUNK, width), lambda b, c: (b, c, 0)),
        out_shape=jax.ShapeDtypeStruct((bsz, seq, width), BF16),
        scratch_shapes=[pltpu.VMEM((SSD_CHUNK + SUBLANES, cw), F32),
                        pltpu.VMEM((SSD_GROUPS, SSD_STATE, SSD_HEADS_PER_GROUP * SSD_HEAD_DIM), F32),
                        pltpu.VMEM((SSD_CHUNK, width), F32)],
        compiler_params=_params("parallel", "arbitrary"),
        name="ssd_mixer",
    )(xbc, dt_raw, z, conv_w, conv_b.reshape(1, cw), dtb, a_neg, dskip, norm_g.reshape(1, width))


def _lru_kernel(x_ref, gate_ref, cw_ref, cb_ref, w_ref, br_ref, bi_ref, lam_ref, o_ref, buf, a_s, u_s, h_s):
    rows = x_ref.shape[1]
    width = x_ref.shape[2]
    half = width // 2

    @pl.when(pl.program_id(1) == 0)
    def _():
        buf[0:SUBLANES, :] = jnp.zeros((SUBLANES, width), F32)
        h_s[...] = jnp.zeros(h_s.shape, F32)

    xc = _causal_conv(buf, x_ref[0], cw_ref, cb_ref, rows)
    sp = _softplus(-lam_ref[...])
    for c in range(2):
        cols = slice(c * half, (c + 1) * half)
        pre = jnp.dot(xc[:, cols].astype(BF16), w_ref[0, c], preferred_element_type=F32)
        r_gate = jax.nn.sigmoid(pre[:, :half] + br_ref[:, cols])
        i_gate = jax.nn.sigmoid(pre[:, half:] + bi_ref[:, cols])
        log_a = -LRU_C * r_gate * sp[:, cols]
        t = jnp.tanh(log_a)
        one_minus_a2 = -2.0 * t / (1.0 - t)
        a_s[:, cols] = jnp.exp(log_a)
        u_s[:, cols] = jnp.sqrt(one_minus_a2) * (i_gate * xc[:, cols])

    sub = lax.broadcasted_iota(I32, (SUBLANES, width), 0)

    def scan_body(i, h):
        tile = pl.ds(pl.multiple_of(i * SUBLANES, SUBLANES), SUBLANES)
        a = a_s[tile, :]
        u = u_s[tile, :]
        for s in (1, 2, 4):
            keep = sub >= s
            a_sh = jnp.where(keep, pltpu.roll(a, s, 0), 1.0)
            u_sh = jnp.where(keep, pltpu.roll(u, s, 0), 0.0)
            u = a * u_sh + u
            a = a * a_sh
        hs = u + a * h
        u_s[tile, :] = hs
        return jnp.broadcast_to(hs[SUBLANES - 1:SUBLANES, :], (SUBLANES, width))

    h_s[...] = lax.fori_loop(0, rows // SUBLANES, scan_body, h_s[...])
    o_ref[0] = (u_s[...] * jax.nn.gelu(gate_ref[0], approximate=True)).astype(o_ref.dtype)


def lru_gate_weights(w_r, w_i):
    depth = w_r.shape[0]
    per_half = LRU_BLOCKS // 2
    half = per_half * LRU_BLOCK_WIDTH

    def block_diag(w):
        eye = jnp.eye(per_half, dtype=w.dtype)
        wb = w.reshape(depth, 2, per_half, LRU_BLOCK_WIDTH, LRU_BLOCK_WIDTH)
        return jnp.einsum('dckij,kl->dckilj', wb, eye).reshape(depth, 2, half, half)

    return jnp.concatenate([block_diag(w_r), block_diag(w_i)], axis=-1).astype(BF16)


def rglru_mixer(lru_gx, conv_w, conv_b, w_cat, b_r, b_i, lam, layer):
    bsz, seq, w2 = lru_gx.shape
    width = w2 // 2
    half = width // 2
    ts = 256
    row = pl.BlockSpec((1, width), lambda b, i: (0, 0))
    return pl.pallas_call(
        _lru_kernel,
        grid=(bsz, seq // ts),
        in_specs=[pl.BlockSpec((1, ts, width), lambda b, i: (b, i, 1)),
                  pl.BlockSpec((1, ts, width), lambda b, i: (b, i, 0)),
                  pl.BlockSpec((CONV_K, width), lambda b, i: (0, 0)),
                  row,
                  pl.BlockSpec((1, 2, half, width), lambda b, i: (layer, 0, 0, 0)),
                  row, row, row],
        out_specs=pl.BlockSpec((1, ts, width), lambda b, i: (b, i, 0)),
        out_shape=jax.ShapeDtypeStruct((bsz, seq, width), BF16),
        scratch_shapes=[pltpu.VMEM((ts + SUBLANES, width), F32), pltpu.VMEM((ts, width), F32),
                        pltpu.VMEM((ts, width), F32), pltpu.VMEM((SUBLANES, width), F32)],
        compiler_params=_params("parallel", "arbitrary"),
        name="rglru_mixer",
    )(lru_gx, lru_gx, conv_w, conv_b.reshape(1, width), w_cat, b_r.reshape(1, width), b_i.reshape(1, width),
      lam.reshape(1, width))


def _merge_kernel(x_ref, ya_ref, ys_ref, yl_ref, mg_ref, g1_ref, wa_ref, ws_ref, wl_ref, wo_ref, o_ref):
    d = x_ref.shape[2]
    gates = jax.nn.sigmoid(mg_ref[0])
    merged = (gates[:, :d] * jnp.dot(ya_ref[0], wa_ref[0], preferred_element_type=F32)
              + gates[:, d:2 * d] * jnp.dot(ys_ref[0], ws_ref[0], preferred_element_type=F32)
              + gates[:, 2 * d:] * jnp.dot(yl_ref[0], wl_ref[0], preferred_element_type=F32))
    y = jnp.dot(merged.astype(BF16), wo_ref[0], preferred_element_type=F32)
    o_ref[0] = x_ref[0] + g1_ref[0] * y


def merge_block(x, y_attn, y_ssd, y_lru, merge, g1, w_a, w_s, w_l, w_o, layer):
    bsz, seq, d = x.shape
    ts = 512
    tok = lambda n: pl.BlockSpec((1, ts, n), lambda b, i: (b, i, 0))
    full = lambda w: pl.BlockSpec((1,) + w.shape[1:], lambda b, i: (layer, 0, 0))
    return pl.pallas_call(
        _merge_kernel,
        grid=(bsz, seq // ts),
        in_specs=[tok(d), tok(y_attn.shape[-1]), tok(y_ssd.shape[-1]), tok(y_lru.shape[-1]), tok(merge.shape[-1]),
                  pl.BlockSpec((1, 1, d), lambda b, i: (b, 0, 0)),
                  full(w_a), full(w_s), full(w_l), full(w_o)],
        out_specs=tok(d),
        out_shape=jax.ShapeDtypeStruct((bsz, seq, d), F32),
        compiler_params=_params("parallel", "parallel"),
        name="merge_block",
    )(x, y_attn, y_ssd, y_lru, merge, g1.reshape(bsz, 1, d), w_a, w_s, w_l, w_o)


ROUTE_E, ROUTE_W, ROUTE_RANK = 0, 2, 4


def _router_kernel(x_ref, g_ref, sc_ref, sh_ref, wr_ref, br_ref, h_ref, route_ref, count_ref, run):
    ts = x_ref.shape[0]

    @pl.when(pl.program_id(0) == 0)
    def _():
        run[...] = jnp.zeros(run.shape, F32)

    x = x_ref[...]
    y = x * lax.rsqrt(jnp.mean(x * x, axis=-1, keepdims=True) + EPS) * g_ref[...]
    h = y * (1.0 + sc_ref[0]) + sh_ref[0]
    h_ref[...] = h
    h_hi = h.astype(BF16)
    r1 = h - h_hi.astype(F32)
    h_mid = r1.astype(BF16)
    h_lo = (r1 - h_mid.astype(F32)).astype(BF16)
    w_hi = wr_ref[0, 0]
    w_lo = wr_ref[0, 1]
    logits = (jnp.dot(h_hi, w_hi, preferred_element_type=F32) + jnp.dot(h_hi, w_lo, preferred_element_type=F32)
              + jnp.dot(h_mid, w_hi, preferred_element_type=F32) + jnp.dot(h_mid, w_lo, preferred_element_type=F32)
              + jnp.dot(h_lo, w_hi, preferred_element_type=F32)) + br_ref[0]
    lane = lax.broadcasted_iota(I32, (ts, LANES), 1)
    big = jnp.int32(LANES)
    lg = jnp.where(lane < MOE_GROUPS, logits, NEG)
    mg = jnp.max(lg, axis=-1, keepdims=True)
    g_sel = jnp.min(jnp.where(lg == mg, lane, big), axis=-1, keepdims=True)
    p_grp = 1.0 / jnp.sum(jnp.exp(lg - mg), axis=-1, keepdims=True)
    lo = MOE_GROUPS + EXPERTS_PER_GROUP * g_sel
    in_grp = (lane >= lo) & (lane < lo + EXPERTS_PER_GROUP)
    le = jnp.where(in_grp, logits, NEG)
    m1 = jnp.max(le, axis=-1, keepdims=True)
    i1 = jnp.min(jnp.where(le == m1, lane, big), axis=-1, keepdims=True)
    den = jnp.sum(jnp.exp(le - m1), axis=-1, keepdims=True)
    le2 = jnp.where(lane == i1, NEG, le)
    m2 = jnp.max(le2, axis=-1, keepdims=True)
    i2 = jnp.min(jnp.where(le2 == m2, lane, big), axis=-1, keepdims=True)
    p1 = 1.0 / den
    p2 = jnp.exp(m2 - m1) / den
    w1 = p_grp * (p1 / (p1 + p2))
    w2 = p_grp * (p2 / (p1 + p2))
    hot1 = lane == i1
    hot2 = lane == i2
    onehot = jnp.where(hot1 | hot2, 1.0, 0.0)
    ri = lax.broadcasted_iota(I32, (ts, ts), 0)
    ci = lax.broadcasted_iota(I32, (ts, ts), 1)
    strict = jnp.where(ri > ci, 1.0, 0.0).astype(BF16)
    before = jnp.dot(strict, onehot.astype(BF16), preferred_element_type=F32) + run[...]
    rank1 = jnp.sum(jnp.where(hot1, before, 0.0), axis=-1, keepdims=True)
    rank2 = jnp.sum(jnp.where(hot2, before, 0.0), axis=-1, keepdims=True)
    run[...] = run[...] + jnp.sum(onehot, axis=0, keepdims=True)
    count_ref[...] = run[...]
    e1 = (i1 - MOE_GROUPS).astype(F32)
    e2 = (i2 - MOE_GROUPS).astype(F32)
    packed = jnp.where(lane == ROUTE_E, e1, 0.0)
    packed = jnp.where(lane == ROUTE_E + 1, e2, packed)
    packed = jnp.where(lane == ROUTE_W, w1, packed)
    packed = jnp.where(lane == ROUTE_W + 1, w2, packed)
    packed = jnp.where(lane == ROUTE_RANK, rank1, packed)
    packed = jnp.where(lane == ROUTE_RANK + 1, rank2, packed)
    route_ref[...] = packed


def router_params(router_wg, router_bg, router_we, router_be):
    depth, d, _ = router_wg.shape
    w_all = jnp.concatenate([router_wg, router_we.reshape(depth, d, N_EXPERTS)], axis=-1)
    w_all = jnp.pad(w_all, ((0, 0), (0, 0), (0, LANES - w_all.shape[-1])))
    w_hi = w_all.astype(BF16)
    w_lo = (w_all - w_hi.astype(F32)).astype(BF16)
    b_all = jnp.concatenate([router_bg, router_be.reshape(depth, N_EXPERTS)], axis=-1)
    b_all = jnp.pad(b_all, ((0, 0), (0, LANES - b_all.shape[-1]))).reshape(depth, 1, LANES)
    return jnp.stack([w_hi, w_lo], axis=1), b_all


def moe_router(x, g, scale, shift, w_stack, b_stack, layer):
    bsz, seq, d = x.shape
    n_tok = bsz * seq
    ts = 512
    per_batch = seq // ts
    return pl.pallas_call(
        _router_kernel,
        grid=(n_tok // ts,),
        in_specs=[pl.BlockSpec((ts, d), lambda i: (i, 0)),
                  pl.BlockSpec((1, d), lambda i: (0, 0)),
                  pl.BlockSpec((1, 1, d), lambda i: (i // per_batch, 0, 0)),
                  pl.BlockSpec((1, 1, d), lambda i: (i // per_batch, 0, 0)),
                  pl.BlockSpec((1, 2, d, LANES), lambda i: (layer, 0, 0, 0)),
                  pl.BlockSpec((1, 1, LANES), lambda i: (layer, 0, 0))],
        out_specs=[pl.BlockSpec((ts, d), lambda i: (i, 0)),
                   pl.BlockSpec((ts, LANES), lambda i: (i, 0)),
                   pl.BlockSpec((1, LANES), lambda i: (0, 0))],
        out_shape=[jax.ShapeDtypeStruct((n_tok, d), F32),
                   jax.ShapeDtypeStruct((n_tok, LANES), F32),
                   jax.ShapeDtypeStruct((1, LANES), F32)],
        scratch_shapes=[pltpu.VMEM((1, LANES), F32)],
        compiler_params=_params("arbitrary"),
        name="moe_router",
    )(x.reshape(n_tok, d), g.reshape(1, d), scale.reshape(bsz, 1, d), shift.reshape(bsz, 1, d), w_stack, b_stack)


DISPATCH_TOKENS = 512


def _dispatch_kernel(d0_ref, d1_ref, h_ref, zero_ref, xd_ref, sem):
    del zero_ref

    def copies(j):
        src = h_ref.at[pl.ds(j, 1), :]
        return (pltpu.make_async_copy(src, xd_ref.at[pl.ds(d0_ref[0, 0, j], 1), :], sem),
                pltpu.make_async_copy(src, xd_ref.at[pl.ds(d1_ref[0, 0, j], 1), :], sem))

    def start(j, carry):
        for cp in copies(j):
            cp.start()
        return carry

    def wait(j, carry):
        for cp in copies(j):
            cp.wait()
        return carry

    lax.fori_loop(0, DISPATCH_TOKENS, start, 0, unroll=8)
    lax.fori_loop(0, DISPATCH_TOKENS, wait, 0, unroll=8)


def moe_dispatch(h, dest, n_rows):
    n_tok, d = h.shape
    tt = DISPATCH_TOKENS
    steps = n_tok // tt
    zeros = jnp.zeros((n_rows, d), h.dtype)
    idx = pl.BlockSpec((1, 1, tt), lambda i: (i, 0, 0), memory_space=pltpu.SMEM)
    return pl.pallas_call(
        _dispatch_kernel,
        grid=(steps,),
        in_specs=[idx, idx,
                  pl.BlockSpec((tt, d), lambda i: (i, 0)),
                  pl.BlockSpec(memory_space=pl.ANY)],
        out_specs=pl.BlockSpec(memory_space=pl.ANY),
        out_shape=jax.ShapeDtypeStruct((n_rows, d), h.dtype),
        input_output_aliases={3: 0},
        scratch_shapes=[pltpu.SemaphoreType.DMA(())],
        compiler_params=pltpu.CompilerParams(dimension_semantics=("arbitrary",), has_side_effects=True),
        name="moe_dispatch",
    )(dest[:, 0].reshape(steps, 1, tt), dest[:, 1].reshape(steps, 1, tt), h, zeros)


def _expert_kernel(be_ref, nb_ref, xd_ref, wg_ref, wu_ref, wd_ref, o_ref, wg_b, wu_b, wd_b):
    i = pl.program_id(0)

    @pl.when(jnp.logical_or(i == 0, be_ref[i] != be_ref[jnp.maximum(i - 1, 0)]))
    def _():
        wg_b[...] = wg_ref[0, 0].astype(BF16)
        wu_b[...] = wu_ref[0, 0].astype(BF16)
        wd_b[...] = wd_ref[0, 0].astype(BF16)

    @pl.when(i < nb_ref[0])
    def _():
        xb = xd_ref[...].astype(BF16)
        gate = jnp.dot(xb, wg_b[...], preferred_element_type=F32)
        up = jnp.dot(xb, wu_b[...], preferred_element_type=F32)
        o_ref[...] = jnp.dot((_silu(gate) * up).astype(BF16), wd_b[...], preferred_element_type=F32)

    @pl.when(i >= nb_ref[0])
    def _():
        o_ref[...] = jnp.zeros(o_ref.shape, F32)


def moe_experts(xd, block_e, n_used, w_gate, w_up, w_down, layer):
    n_rows, d = xd.shape
    ff = w_gate.shape[-1]
    n_blocks = n_rows // MOE_BLOCK
    grid_spec = pltpu.PrefetchScalarGridSpec(
        num_scalar_prefetch=2,
        grid=(n_blocks,),
        in_specs=[pl.BlockSpec((MOE_BLOCK, d), lambda i, be, nb: (i, 0)),
                  pl.BlockSpec((1, 1, d, ff), lambda i, be, nb: (layer, be[i], 0, 0)),
                  pl.BlockSpec((1, 1, d, ff), lambda i, be, nb: (layer, be[i], 0, 0)),
                  pl.BlockSpec((1, 1, ff, d), lambda i, be, nb: (layer, be[i], 0, 0))],
        out_specs=pl.BlockSpec((MOE_BLOCK, d), lambda i, be, nb: (i, 0)),
        scratch_shapes=[pltpu.VMEM((d, ff), BF16), pltpu.VMEM((d, ff), BF16), pltpu.VMEM((ff, d), BF16)],
    )
    return pl.pallas_call(
        _expert_kernel,
        grid_spec=grid_spec,
        out_shape=jax.ShapeDtypeStruct((n_rows, d), F32),
        compiler_params=_params("arbitrary"),
        name="moe_experts",
    )(block_e, n_used, xd, w_gate, w_up, w_down)


COMBINE_TOKENS = 256


def _combine_kernel(d0_ref, d1_ref, x_ref, route_ref, g2_ref, yd_ref, o_ref, rows, sem):
    def copies(j):
        return (pltpu.make_async_copy(yd_ref.at[pl.ds(d0_ref[0, 0, j], 1), :], rows.at[0, pl.ds(j, 1), :], sem),
                pltpu.make_async_copy(yd_ref.at[pl.ds(d1_ref[0, 0, j], 1), :], rows.at[1, pl.ds(j, 1), :], sem))

    def start(j, carry):
        for cp in copies(j):
            cp.start()
        return carry

    def wait(j, carry):
        for cp in copies(j):
            cp.wait()
        return carry

    lax.fori_loop(0, COMBINE_TOKENS, start, 0, unroll=8)
    lax.fori_loop(0, COMBINE_TOKENS, wait, 0, unroll=8)
    route = route_ref[...]
    w1 = route[:, ROUTE_W:ROUTE_W + 1]
    w2 = route[:, ROUTE_W + 1:ROUTE_W + 2]
    y = rows[0] * w1 + rows[1] * w2
    o_ref[...] = x_ref[...] + g2_ref[0] * y


def moe_combine(x, route, g2, yd, dest):
    bsz, seq, d = x.shape
    n_tok = bsz * seq
    tt = COMBINE_TOKENS
    per_batch = seq // tt
    steps = n_tok // tt
    idx = pl.BlockSpec((1, 1, tt), lambda i: (i, 0, 0), memory_space=pltpu.SMEM)
    return pl.pallas_call(
        _combine_kernel,
        grid=(steps,),
        in_specs=[idx, idx,
                  pl.BlockSpec((tt, d), lambda i: (i, 0)),
                  pl.BlockSpec((tt, LANES), lambda i: (i, 0)),
                  pl.BlockSpec((1, 1, d), lambda i: (i // per_batch, 0, 0)),
                  pl.BlockSpec(memory_space=pl.ANY)],
        out_specs=pl.BlockSpec((tt, d), lambda i: (i, 0)),
        out_shape=jax.ShapeDtypeStruct((n_tok, d), F32),
        scratch_shapes=[pltpu.VMEM((TOP_K, tt, d), F32), pltpu.SemaphoreType.DMA(())],
        compiler_params=_params("arbitrary"),
        name="moe_combine",
    )(dest[:, 0].reshape(steps, 1, tt), dest[:, 1].reshape(steps, 1, tt), x.reshape(n_tok, d), route,
      g2.reshape(bsz, 1, d), yd).reshape(bsz, seq, d)


def hier_moe_block(x, g, scale, shift, gate2, router_w, router_b, w_gate, w_up, w_down, layer):
    bsz, seq, d = x.shape
    n_tok = bsz * seq
    n_asg = n_tok * TOP_K
    n_blocks = -(-n_asg // MOE_BLOCK) + N_EXPERTS
    h, route, counts = moe_router(x, g, scale, shift, router_w, router_b, layer)
    counts = counts[0, MOE_GROUPS:MOE_GROUPS + N_EXPERTS].astype(I32)
    pcounts = ((counts + MOE_BLOCK - 1) // MOE_BLOCK) * MOE_BLOCK
    pend = jnp.cumsum(pcounts)
    pstart = pend - pcounts
    expert = route[:, ROUTE_E:ROUTE_E + TOP_K].astype(I32)
    rank = route[:, ROUTE_RANK:ROUTE_RANK + TOP_K].astype(I32)
    experts = jnp.arange(N_EXPERTS, dtype=I32)
    dest = rank + jnp.sum(jnp.where(expert[..., None] == experts, pstart, 0), axis=-1)
    block_rows = jnp.arange(n_blocks, dtype=I32) * MOE_BLOCK
    block_e = jnp.minimum(jnp.sum((pend[None, :] <= block_rows[:, None]).astype(I32), axis=1), N_EXPERTS - 1)
    n_used = (pend[-1:] // MOE_BLOCK).astype(I32)
    xd = moe_dispatch(h, dest, n_blocks * MOE_BLOCK)
    yd = moe_experts(xd, block_e, n_used, w_gate, w_up, w_down, layer)
    return moe_combine(x, route, gate2, yd, dest)


def kernel(x, c, positions, ada_w, ada_b, norm1_g, norm2_g, w_in, ssd_conv_w, ssd_conv_b, ssd_dt_bias, ssd_a_log, ssd_d, ssd_norm_g, lru_conv_w, lru_conv_b, lru_w_r, lru_b_r, lru_w_i, lru_b_i, lru_lambda, w_br_attn, w_br_ssd, w_br_lru, w_out, router_wg, router_bg, router_we, router_be, exp_w_gate, exp_w_up, exp_w_down, final_g):
    depth = ada_w.shape[0]
    bsz, seq, d = x.shape
    n_tok = bsz * seq
    ssd_width = ssd_norm_g.shape[-1]
    xbc_width = ssd_conv_w.shape[-1]
    ssd_heads = ssd_dt_bias.shape[-1]
    lru_width = lru_lambda.shape[-1]
    o_z = 3 * ATTN_WIDTH
    o_xbc = o_z + ssd_width
    o_dt = o_xbc + xbc_width
    o_lru = o_dt + ssd_heads
    o_merge = o_lru + 2 * lru_width
    tn_big, tn_qkv, tn_lru = 1024, ATTN_WIDTH, lru_width
    c_xbc, c_z = 0, xbc_width
    c_merge = c_z + ssd_width
    c_qkv = c_merge + 3 * d
    c_lru = -(-(c_qkv + o_z) // tn_lru) * tn_lru
    c_dt = c_lru + 2 * lru_width
    seg = lambda a, b: w_in[:, :, a:b]
    zeros = lambda n: jnp.zeros(w_in.shape[:2] + (n,), w_in.dtype)
    w_all = jnp.concatenate([seg(o_xbc, o_dt), seg(o_z, o_xbc), seg(o_merge, w_in.shape[-1]), seg(0, o_z),
                             zeros(c_lru - c_qkv - o_z), seg(o_lru, o_merge), seg(o_dt, o_lru),
                             zeros(LANES - ssd_heads)], axis=-1).astype(BF16)
    w_a, w_s, w_l, w_o = (w.astype(BF16) for w in (w_br_attn, w_br_ssd, w_br_lru, w_out))
    lru_w = lru_gate_weights(lru_w_r, lru_w_i)
    router_w, router_b = router_params(router_wg, router_bg, router_we, router_be)

    cond = ada_cond(c, ada_w, ada_b)
    cos, sin = rope_tables(positions)
    for l in range(depth):
        sh1, sc1, g1, sh2, sc2, g2 = [cond[l, :, i * d:(i + 1) * d] for i in range(6)]
        h = norm_modulate(x, norm1_g[l], sc1, sh1, BF16)
        h2d = h.reshape(n_tok, d)
        qkv = matmul_slabs(h, w_all, l, c_qkv, o_z, tn_qkv)
        z = matmul(h2d, w_all, l, c_z, ssd_width, 2048, tn_big)
        xbc = matmul(h2d, w_all, l, c_xbc, xbc_width, 2048, tn_big)
        dt_raw = matmul(h2d, w_all, l, c_dt, LANES, 2048, LANES)
        lru_gx = matmul(h2d, w_all, l, c_lru, 2 * lru_width, 2048, tn_lru)
        merge = matmul(h2d, w_all, l, c_merge, 3 * d, 2048, tn_big)
        y_attn = dilated_attention(qkv, cos, sin)
        y_ssd = ssd_mixer(xbc.reshape(bsz, seq, -1), dt_raw.reshape(bsz, seq, -1), z.reshape(bsz, seq, -1),
                          ssd_conv_w[l], ssd_conv_b[l], ssd_dt_bias[l], ssd_a_log[l], ssd_d[l], ssd_norm_g[l])
        y_lru = rglru_mixer(lru_gx.reshape(bsz, seq, -1), lru_conv_w[l], lru_conv_b[l], lru_w, lru_b_r[l],
                            lru_b_i[l], lru_lambda[l], l)
        x = merge_block(x, y_attn, y_ssd, y_lru, merge.reshape(bsz, seq, -1), g1, w_a, w_s, w_l, w_o, l)
        x = hier_moe_block(x, norm2_g[l], sc2, sh2, g2, router_w, router_b, exp_w_gate, exp_w_up, exp_w_down, l)
    return final_norm(x, final_g)
```

```python
import functools
import math

import jax
import jax.numpy as jnp
import numpy as np
from jax import lax
from jax.experimental import pallas as pl
from jax.experimental.pallas import tpu as pltpu

F32 = jnp.float32
BF16 = jnp.bfloat16
I32 = jnp.int32

LANES = 128
SUBLANES = 8
VMEM_BYTES = 64 << 20
VMEM_LIMIT = (VMEM_BYTES * 3) // 4

EPS = 1e-6
HEAD_DIM = 64
ATTN_DILATIONS = (1, 4, 16)
ATTN_BLK = 128
ATTN_UNROLL = 4
HEADS_PER_GROUP = 8
N_ATTN_GROUPS = 3
ATTN_WIDTH = N_ATTN_GROUPS * HEADS_PER_GROUP * HEAD_DIM
ROPE_THETA = 10000.0
SSD_HEAD_DIM = 64
SSD_GROUPS = 8
SSD_HEADS_PER_GROUP = 4
SSD_STATE = 128
SSD_CHUNK = 128
CONV_K = 4
LRU_BLOCKS = 16
LRU_BLOCK_WIDTH = 80
LRU_C = 8.0
MOE_GROUPS = 4
EXPERTS_PER_GROUP = 8
N_EXPERTS = MOE_GROUPS * EXPERTS_PER_GROUP
TOP_K = 2
MOE_BLOCK = 512
NEG = -1e30


def _params(*sem):
    return pltpu.CompilerParams(dimension_semantics=sem, vmem_limit_bytes=VMEM_LIMIT)


def _sigmoid(x):
    return 0.5 * jnp.tanh(0.5 * x) + 0.5


def _silu(x):
    return x * _sigmoid(x)


def _softplus(x):
    return jnp.maximum(x, 0.0) + jnp.log1p(jnp.exp(-jnp.abs(x)))


def _ada_kernel(c_ref, w_ref, b_ref, o_ref):
    c_act = _silu(c_ref[...]).astype(BF16)
    o_ref[0] = jnp.dot(c_act, w_ref[0].astype(BF16), preferred_element_type=F32) + b_ref[0]


def ada_cond(c, ada_w, ada_b):
    depth, d_model, n = ada_w.shape
    bsz = c.shape[0]
    tn = 2048
    return pl.pallas_call(
        _ada_kernel,
        grid=(depth, n // tn),
        in_specs=[pl.BlockSpec((bsz, d_model), lambda l, j: (0, 0)),
                  pl.BlockSpec((1, d_model, tn), lambda l, j: (l, 0, j)),
                  pl.BlockSpec((1, 1, tn), lambda l, j: (l, 0, j))],
        out_specs=pl.BlockSpec((1, bsz, tn), lambda l, j: (l, 0, j)),
        out_shape=jax.ShapeDtypeStruct((depth, bsz, n), F32),
        compiler_params=_params("parallel", "parallel"),
        name="ada_cond",
    )(c, ada_w, ada_b.reshape(depth, 1, n))


def _norm_kernel(x_ref, g_ref, sc_ref, sh_ref, o_ref):
    x = x_ref[0]
    y = x * lax.rsqrt(jnp.mean(x * x, axis=-1, keepdims=True) + EPS) * g_ref[...]
    o_ref[0] = (y * (1.0 + sc_ref[0]) + sh_ref[0]).astype(o_ref.dtype)


def _plain_norm_kernel(x_ref, g_ref, o_ref):
    x = x_ref[0]
    o_ref[0] = x * lax.rsqrt(jnp.mean(x * x, axis=-1, keepdims=True) + EPS) * g_ref[...]


def norm_modulate(x, g, scale, shift, out_dtype):
    bsz, seq, d = x.shape
    ts = 512
    return pl.pallas_call(
        _norm_kernel,
        grid=(bsz, seq // ts),
        in_specs=[pl.BlockSpec((1, ts, d), lambda b, i: (b, i, 0)),
                  pl.BlockSpec((1, d), lambda b, i: (0, 0)),
                  pl.BlockSpec((1, 1, d), lambda b, i: (b, 0, 0)),
                  pl.BlockSpec((1, 1, d), lambda b, i: (b, 0, 0))],
        out_specs=pl.BlockSpec((1, ts, d), lambda b, i: (b, i, 0)),
        out_shape=jax.ShapeDtypeStruct((bsz, seq, d), out_dtype),
        compiler_params=_params("parallel", "parallel"),
        name="norm_modulate",
    )(x, g.reshape(1, d), scale.reshape(bsz, 1, d), shift.reshape(bsz, 1, d))


def final_norm(x, g):
    bsz, seq, d = x.shape
    ts = 512
    return pl.pallas_call(
        _plain_norm_kernel,
        grid=(bsz, seq // ts),
        in_specs=[pl.BlockSpec((1, ts, d), lambda b, i: (b, i, 0)),
                  pl.BlockSpec((1, d), lambda b, i: (0, 0))],
        out_specs=pl.BlockSpec((1, ts, d), lambda b, i: (b, i, 0)),
        out_shape=jax.ShapeDtypeStruct((bsz, seq, d), F32),
        compiler_params=_params("parallel", "parallel"),
        name="final_norm",
    )(x, g.reshape(1, d))


def _mm_kernel(a_ref, w_ref, o_ref):
    o_ref[...] = jnp.dot(a_ref[...], w_ref[0], preferred_element_type=F32).astype(o_ref.dtype)


def matmul(a, w_stack, layer, col0, n, tm, tn, out_dtype=F32):
    m, k = a.shape
    j0 = col0 // tn
    assert col0 % tn == 0 and n % tn == 0 and m % tm == 0
    return pl.pallas_call(
        _mm_kernel,
        grid=(m // tm, n // tn),
        in_specs=[pl.BlockSpec((tm, k), lambda i, j: (i, 0)),
                  pl.BlockSpec((1, k, tn), lambda i, j: (layer, 0, j0 + j))],
        out_specs=pl.BlockSpec((tm, tn), lambda i, j: (i, j)),
        out_shape=jax.ShapeDtypeStruct((m, n), out_dtype),
        compiler_params=_params("parallel", "parallel"),
        name="matmul",
    )(a, w_stack)


def _mm_slab_kernel(a_ref, w_ref, o_ref):
    res = jnp.dot(a_ref[0], w_ref[0], preferred_element_type=F32)
    for s in range(o_ref.shape[1]):
        o_ref[0, s] = res[:, s * LANES:(s + 1) * LANES]


def matmul_slabs(a, w_stack, layer, col0, n, tn):
    bsz, seq, k = a.shape
    j0 = col0 // tn
    assert col0 % tn == 0 and n % tn == 0
    return pl.pallas_call(
        _mm_slab_kernel,
        grid=(bsz, n // tn),
        in_specs=[pl.BlockSpec((1, seq, k), lambda b, j: (b, 0, 0)),
                  pl.BlockSpec((1, k, tn), lambda b, j: (layer, 0, j0 + j))],
        out_specs=pl.BlockSpec((1, tn // LANES, seq, LANES), lambda b, j: (b, j, 0, 0)),
        out_shape=jax.ShapeDtypeStruct((bsz, n // LANES, seq, LANES), F32),
        compiler_params=_params("parallel", "parallel"),
        name="matmul_slabs",
    )(a, w_stack)


def _rope_table_kernel(pos_ref, freq_ref, sign_ref, cos_ref, sin_ref):
    ang = pos_ref[0] * freq_ref[...]
    cos_ref[0] = jnp.cos(ang)
    sin_ref[0] = jnp.sin(ang) * sign_ref[...]


def rope_slab_layout(w):
    half = HEAD_DIM // 2
    lead = w.shape[:-1]
    w = w.reshape(lead + (w.shape[-1] // LANES, 2, 2, half))
    return jnp.swapaxes(w, -2, -3).reshape(lead + (-1,))


def rope_tables(positions):
    bsz, seq = positions.shape
    half = HEAD_DIM // 2
    freqs = ROPE_THETA ** (-jnp.arange(half, dtype=F32) / half)
    freq_row = jnp.tile(freqs, LANES // half).reshape(1, LANES)
    sign_row = jnp.concatenate([-jnp.ones((LANES // 2,), F32), jnp.ones((LANES // 2,), F32)])
    pos = jnp.broadcast_to(positions.astype(F32)[..., None], (bsz, seq, LANES))
    ts = 512
    spec = pl.BlockSpec((1, ts, LANES), lambda b, i: (b, i, 0))
    row = pl.BlockSpec((1, LANES), lambda b, i: (0, 0))
    return pl.pallas_call(
        _rope_table_kernel,
        grid=(bsz, seq // ts),
        in_specs=[spec, row, row],
        out_specs=[spec, spec],
        out_shape=[jax.ShapeDtypeStruct((bsz, seq, LANES), F32)] * 2,
        compiler_params=_params("parallel", "parallel"),
        name="rope_tables",
    )(pos, freq_row, sign_row.reshape(1, LANES))


def _attn_kernel(q1, k1, v1, q2, k2, v2, q3, k3, v3, cos_ref, sin_ref, o_ref, qr, kr, vr, og, lg):
    seq = o_ref.shape[1]
    blk = ATTN_BLK
    pad = kr.shape[0] - seq
    rope_rows = 256
    lane_b = lax.broadcasted_iota(I32, (blk, LANES), 1)
    head0 = lane_b < HEAD_DIM
    qk_head0 = (lane_b % HEAD_DIM) < (HEAD_DIM // 2)
    qi = lax.broadcasted_iota(I32, (2 * blk, 2 * blk), 0) % blk
    ki = lax.broadcasted_iota(I32, (2 * blk, 2 * blk), 1)
    bias_prev = jnp.where(ki >= qi, 0.0, NEG).astype(F32)
    bias_cur = jnp.where(ki - blk <= qi, 0.0, NEG).astype(F32)
    bias_inner = jnp.where(ki < blk, bias_prev, bias_cur)
    bias_first = jnp.where(ki < blk, NEG, bias_cur)
    nt = (((1,), (1,)), ((), ()))
    ones = jnp.ones((2 * blk, LANES), BF16)
    kr[0:pad, :] = jnp.zeros((pad, LANES), F32)
    vr[0:pad, :] = jnp.zeros((pad, LANES), F32)

    for g, (q_ref, k_ref, v_ref) in enumerate(((q1, k1, v1), (q2, k2, v2), (q3, k3, v3))):
        dil = ATTN_DILATIONS[g]
        n_blk = seq // (dil * blk)

        def rope_body(i, carry, q_ref=q_ref, k_ref=k_ref, v_ref=v_ref):
            rows = pl.ds(pl.multiple_of(i * rope_rows, rope_rows), rope_rows)
            rows_p = pl.ds(pl.multiple_of(pad + i * rope_rows, rope_rows), rope_rows)
            cos = cos_ref[0, rows, :]
            sin = sin_ref[0, rows, :]
            for src, scale in ((q_ref, HEAD_DIM ** -0.5), (k_ref, None)):
                t = src[0, 0, rows, :]
                r = t * cos + pltpu.roll(t, LANES // 2, 1) * sin
                if scale is None:
                    kr[rows_p, :] = r
                else:
                    qr[rows, :] = r * scale
            vr[rows_p, :] = v_ref[0, 0, rows, :]
            return carry

        lax.fori_loop(0, seq // rope_rows, rope_body, 0)

        def blk_body(it, carry, dil=dil, n_blk=n_blk, g=g):
            blocks = []
            for u in range(ATTN_UNROLL):
                idx = it * ATTN_UNROLL + u
                phase = idx // n_blk
                n = idx % n_blk
                start = phase + dil * blk * n
                cur = pl.ds(start, blk, stride=dil) if dil > 1 else pl.ds(start, blk)
                both = (pl.ds(pad + start - dil * blk, 2 * blk, stride=dil) if dil > 1
                        else pl.ds(pad + start - blk, 2 * blk))
                blocks.append((cur, both, n))
            scores = []
            for cur, both, n in blocks:
                q = qr[cur, :]
                q2 = jnp.concatenate([jnp.where(qk_head0, q, 0.0), jnp.where(qk_head0, 0.0, q)], axis=0).astype(BF16)
                s = lax.dot_general(q2, kr[both, :].astype(BF16), nt, preferred_element_type=F32)
                scores.append(s + jnp.where(n > 0, bias_inner, bias_first))
            probs = []
            for s in scores:
                m = jnp.max(jnp.maximum(s[:, :blk], s[:, blk:]), axis=-1, keepdims=True)
                probs.append((m, jnp.exp(s - m).astype(BF16)))
            for (cur, both, _), (m, p) in zip(blocks, probs):
                v1 = jnp.concatenate([vr[both, :].astype(BF16), ones], axis=1)
                r = jnp.dot(p, v1, preferred_element_type=F32)
                den = jnp.where(head0, r[:blk, LANES:], r[blk:, LANES:])
                og[g, cur, :] = jnp.where(head0, r[:blk, :LANES], r[blk:, :LANES]) / den
                m_b = jnp.broadcast_to(m, (2 * blk, LANES))
                lg[g, cur, :] = jnp.where(head0, m_b[:blk], m_b[blk:]) + jnp.log(den)
            return carry

        lax.fori_loop(0, dil * n_blk // ATTN_UNROLL, blk_body, 0)

    def mix_body(i, carry):
        rows = pl.ds(pl.multiple_of(i * rope_rows, rope_rows), rope_rows)
        l0, l1, l2 = lg[0, rows, :], lg[1, rows, :], lg[2, rows, :]
        m = jnp.maximum(jnp.maximum(l0, l1), l2)
        e0, e1, e2 = jnp.exp(l0 - m), jnp.exp(l1 - m), jnp.exp(l2 - m)
        tot = e0 + e1 + e2
        y = (e0 / tot) * og[0, rows, :] + (e1 / tot) * og[1, rows, :] + (e2 / tot) * og[2, rows, :]
        o_ref[0, rows, :] = y.astype(o_ref.dtype)
        return carry

    lax.fori_loop(0, seq // rope_rows, mix_body, 0)


def dilated_attention(qkv, cos, sin):
    bsz, n_slabs, seq, _ = qkv.shape
    pairs = HEADS_PER_GROUP * HEAD_DIM // LANES
    per_proj = n_slabs // 3
    pad = ATTN_BLK * max(ATTN_DILATIONS)

    def slab(proj, g):
        return pl.BlockSpec((1, 1, seq, LANES), lambda b, j: (b, proj * per_proj + g * pairs + j, 0, 0))

    in_specs = []
    for g in range(N_ATTN_GROUPS):
        in_specs += [slab(0, g), slab(1, g), slab(2, g)]
    table = pl.BlockSpec((1, seq, LANES), lambda b, j: (b, 0, 0))
    in_specs += [table, table]
    return pl.pallas_call(
        _attn_kernel,
        grid=(bsz, pairs),
        in_specs=in_specs,
        out_specs=pl.BlockSpec((1, seq, LANES), lambda b, j: (b, 0, j)),
        out_shape=jax.ShapeDtypeStruct((bsz, seq, pairs * LANES), BF16),
        scratch_shapes=[pltpu.VMEM((seq, LANES), F32), pltpu.VMEM((pad + seq, LANES), F32),
                        pltpu.VMEM((pad + seq, LANES), F32),
                        pltpu.VMEM((N_ATTN_GROUPS, seq, LANES), F32), pltpu.VMEM((N_ATTN_GROUPS, seq, LANES), F32)],
        compiler_params=_params("parallel", "parallel"),
        name="dilated_attention",
    )(*([qkv] * 9), cos, sin)


def _causal_conv(buf, x, w_ref, b_ref, rows):
    buf[SUBLANES:SUBLANES + rows, :] = x
    acc = b_ref[...] + w_ref[CONV_K - 1:CONV_K, :] * x
    for k in range(CONV_K - 1):
        off = SUBLANES - (CONV_K - 1) + k
        acc = acc + w_ref[k:k + 1, :] * buf[off:off + rows, :]
    buf[0:SUBLANES, :] = buf[rows:rows + SUBLANES, :]
    return acc


def _by_head(cols, lane, width):
    out = cols[-1]
    for j in range(len(cols) - 2, -1, -1):
        out = jnp.where(lane < (j + 1) * width, cols[j], out)
    return out


def _ssd_kernel(xbc_ref, dt_ref, z_ref, cw_ref, cb_ref, dtb_ref, a_ref, dskip_ref, ng_ref, o_ref,
                buf, state, ybuf):
    chunk = SSD_CHUNK
    width = SSD_GROUPS * SSD_HEADS_PER_GROUP * SSD_HEAD_DIM
    gw = SSD_HEADS_PER_GROUP * SSD_HEAD_DIM

    @pl.when(pl.program_id(1) == 0)
    def _():
        buf[0:SUBLANES, :] = jnp.zeros((SUBLANES, buf.shape[1]), F32)
        state[...] = jnp.zeros(state.shape, F32)

    xbc = _silu(_causal_conv(buf, xbc_ref[0], cw_ref, cb_ref, chunk))
    dt = _softplus(dt_ref[0] + dtb_ref[...])
    a = dt * a_ref[...]
    li = lax.broadcasted_iota(I32, (chunk, chunk), 0)
    si = lax.broadcasted_iota(I32, (chunk, chunk), 1)
    tril = li >= si
    tri = jnp.where(tril, 1.0, 0.0).astype(BF16)
    a_hi = a.astype(BF16)
    r1 = a - a_hi.astype(F32)
    a_mid = r1.astype(BF16)
    a_lo = (r1 - a_mid.astype(F32)).astype(BF16)
    a_cs = (jnp.dot(tri, a_hi, preferred_element_type=F32) + jnp.dot(tri, a_mid, preferred_element_type=F32)
            + jnp.dot(tri, a_lo, preferred_element_type=F32))
    a_cs_t = a_cs.T
    a_last = a_cs[chunk - 1:chunk, :]
    e_acs = jnp.exp(a_cs)
    e_dec = jnp.exp(a_last - a_cs)
    e_last = jnp.exp(a_last)
    lane_g = lax.broadcasted_iota(I32, (chunk, gw), 1)
    lane_1 = lax.broadcasted_iota(I32, (1, gw), 1)

    for g in range(SSD_GROUPS):
        heads = range(g * SSD_HEADS_PER_GROUP, (g + 1) * SSD_HEADS_PER_GROUP)
        xs = xbc[:, g * gw:(g + 1) * gw]
        bm = xbc[:, width + g * SSD_STATE: width + (g + 1) * SSD_STATE]
        cm = xbc[:, width + SSD_GROUPS * SSD_STATE + g * SSD_STATE: width + SSD_GROUPS * SSD_STATE + (g + 1) * SSD_STATE]
        bm_t = bm.T.astype(BF16)
        cm_b = cm.astype(BF16)
        cb = jnp.dot(cm_b, bm_t, preferred_element_type=F32)
        dt_g = _by_head([dt[:, h:h + 1] for h in heads], lane_g, SSD_HEAD_DIM)
        x_dt = xs * dt_g
        x_dt_b = x_dt.astype(BF16)
        diag = []
        for h in heads:
            l_mat = jnp.where(tril, jnp.exp(a_cs[:, h:h + 1] - a_cs_t[h:h + 1, :]), 0.0)
            diag.append(jnp.dot((cb * l_mat).astype(BF16), x_dt_b, preferred_element_type=F32))
        y = _by_head(diag, lane_g, SSD_HEAD_DIM)
        s_prev = state[g]
        y = y + jnp.dot(cm_b, s_prev.astype(BF16), preferred_element_type=F32) * _by_head(
            [e_acs[:, h:h + 1] for h in heads], lane_g, SSD_HEAD_DIM)
        x_dec = (x_dt * _by_head([e_dec[:, h:h + 1] for h in heads], lane_g, SSD_HEAD_DIM)).astype(BF16)
        state[g] = (s_prev * _by_head([e_last[:, h:h + 1] for h in heads], lane_1, SSD_HEAD_DIM)
                    + jnp.dot(bm_t, x_dec, preferred_element_type=F32))
        ybuf[:, g * gw:(g + 1) * gw] = y + dskip_ref[:, g * gw:(g + 1) * gw] * xs

    v = ybuf[...] * _silu(z_ref[0])
    o_ref[0] = (v * lax.rsqrt(jnp.mean(v * v, axis=-1, keepdims=True) + EPS) * ng_ref[...]).astype(o_ref.dtype)


def ssd_mixer(xbc, dt_raw, z, conv_w, conv_b, dt_bias, a_log, d_skip, norm_g):
    bsz, seq, cw = xbc.shape
    width = z.shape[-1]
    heads = dt_bias.shape[0]
    pad = LANES - heads
    dtb = jnp.pad(dt_bias, (0, pad)).reshape(1, LANES)
    a_neg = jnp.pad(-jnp.exp(a_log), (0, pad)).reshape(1, LANES)
    dskip = jnp.repeat(d_skip, SSD_HEAD_DIM).reshape(1, width)
    row = lambda n: pl.BlockSpec((1, n), lambda b, c: (0, 0))
    return pl.pallas_call(
        _ssd_kernel,
        grid=(bsz, seq // SSD_CHUNK),
        in_specs=[pl.BlockSpec((1, SSD_CHUNK, cw), lambda b, c: (b, c, 0)),
                  pl.BlockSpec((1, SSD_CHUNK, LANES), lambda b, c: (b, c, 0)),
                  pl.BlockSpec((1, SSD_CHUNK, width), lambda b, c: (b, c, 0)),
                  pl.BlockSpec((CONV_K, cw), lambda b, c: (0, 0)),
                  row(cw), row(LANES), row(LANES), row(width), row(width)],
        out_specs=pl.BlockSpec((1, SSD_CHUNK, width), lambda b, c: (b, c, 0)),
        out_shape=jax.ShapeDtypeStruct((bsz, seq, width), BF16),
        scratch_shapes=[pltpu.VMEM((SSD_CHUNK + SUBLANES, cw), F32),
                        pltpu.VMEM((SSD_GROUPS, SSD_STATE, SSD_HEADS_PER_GROUP * SSD_HEAD_DIM), F32),
                        pltpu.VMEM((SSD_CHUNK, width), F32)],
        compiler_params=_params("parallel", "arbitrary"),
        name="ssd_mixer",
    )(xbc, dt_raw, z, conv_w, conv_b.reshape(1, cw), dtb, a_neg, dskip, norm_g.reshape(1, width))


def _lru_kernel(x_ref, gate_ref, cw_ref, cb_ref, w_ref, br_ref, bi_ref, lam_ref, o_ref, buf, a_s, u_s, h_s):
    rows = x_ref.shape[1]
    width = x_ref.shape[2]
    half = width // 2

    @pl.when(pl.program_id(1) == 0)
    def _():
        buf[0:SUBLANES, :] = jnp.zeros((SUBLANES, width), F32)
        h_s[...] = jnp.zeros(h_s.shape, F32)

    xc = _causal_conv(buf, x_ref[0], cw_ref, cb_ref, rows)
    sp = _softplus(-lam_ref[...])
    for c in range(2):
        cols = slice(c * half, (c + 1) * half)
        pre = jnp.dot(xc[:, cols].astype(BF16), w_ref[0, c], preferred_element_type=F32)
        r_gate = _sigmoid(pre[:, :half] + br_ref[:, cols])
        i_gate = _sigmoid(pre[:, half:] + bi_ref[:, cols])
        log_a = -LRU_C * r_gate * sp[:, cols]
        t = jnp.tanh(log_a)
        one_minus_a2 = -2.0 * t / (1.0 - t)
        a_s[:, cols] = jnp.exp(log_a)
        u_s[:, cols] = jnp.sqrt(one_minus_a2) * (i_gate * xc[:, cols])

    sub = lax.broadcasted_iota(I32, (SUBLANES, width), 0)

    def scan_body(i, h):
        tile = pl.ds(pl.multiple_of(i * SUBLANES, SUBLANES), SUBLANES)
        a = a_s[tile, :]
        u = u_s[tile, :]
        for s in (1, 2, 4):
            keep = sub >= s
            a_sh = jnp.where(keep, pltpu.roll(a, s, 0), 1.0)
            u_sh = jnp.where(keep, pltpu.roll(u, s, 0), 0.0)
            u = a * u_sh + u
            a = a * a_sh
        hs = u + a * h
        u_s[tile, :] = hs
        return jnp.broadcast_to(hs[SUBLANES - 1:SUBLANES, :], (SUBLANES, width))

    h_s[...] = lax.fori_loop(0, rows // SUBLANES, scan_body, h_s[...])
    o_ref[0] = (u_s[...] * jax.nn.gelu(gate_ref[0], approximate=True)).astype(o_ref.dtype)


def lru_gate_weights(w_r, w_i):
    depth = w_r.shape[0]
    per_half = LRU_BLOCKS // 2
    half = per_half * LRU_BLOCK_WIDTH

    def block_diag(w):
        eye = jnp.eye(per_half, dtype=w.dtype)
        wb = w.reshape(depth, 2, per_half, LRU_BLOCK_WIDTH, LRU_BLOCK_WIDTH)
        return jnp.einsum('dckij,kl->dckilj', wb, eye).reshape(depth, 2, half, half)

    return jnp.concatenate([block_diag(w_r), block_diag(w_i)], axis=-1).astype(BF16)


def rglru_mixer(lru_gx, conv_w, conv_b, w_cat, b_r, b_i, lam, layer):
    bsz, seq, w2 = lru_gx.shape
    width = w2 // 2
    half = width // 2
    ts = 256
    row = pl.BlockSpec((1, width), lambda b, i: (0, 0))
    return pl.pallas_call(
        _lru_kernel,
        grid=(bsz, seq // ts),
        in_specs=[pl.BlockSpec((1, ts, width), lambda b, i: (b, i, 1)),
                  pl.BlockSpec((1, ts, width), lambda b, i: (b, i, 0)),
                  pl.BlockSpec((CONV_K, width), lambda b, i: (0, 0)),
                  row,
                  pl.BlockSpec((1, 2, half, width), lambda b, i: (layer, 0, 0, 0)),
                  row, row, row],
        out_specs=pl.BlockSpec((1, ts, width), lambda b, i: (b, i, 0)),
        out_shape=jax.ShapeDtypeStruct((bsz, seq, width), BF16),
        scratch_shapes=[pltpu.VMEM((ts + SUBLANES, width), F32), pltpu.VMEM((ts, width), F32),
                        pltpu.VMEM((ts, width), F32), pltpu.VMEM((SUBLANES, width), F32)],
        compiler_params=_params("parallel", "arbitrary"),
        name="rglru_mixer",
    )(lru_gx, lru_gx, conv_w, conv_b.reshape(1, width), w_cat, b_r.reshape(1, width), b_i.reshape(1, width),
      lam.reshape(1, width))


def _merge_kernel(x_ref, ya_ref, ys_ref, yl_ref, mg_ref, g1_ref, wa_ref, ws_ref, wl_ref, wo_ref, o_ref):
    d = x_ref.shape[2]
    gates = _sigmoid(mg_ref[0])
    merged = (gates[:, :d] * jnp.dot(ya_ref[0], wa_ref[0], preferred_element_type=F32)
              + gates[:, d:2 * d] * jnp.dot(ys_ref[0], ws_ref[0], preferred_element_type=F32)
              + gates[:, 2 * d:] * jnp.dot(yl_ref[0], wl_ref[0], preferred_element_type=F32))
    y = jnp.dot(merged.astype(BF16), wo_ref[0], preferred_element_type=F32)
    o_ref[0] = x_ref[0] + g1_ref[0] * y


def merge_block(x, y_attn, y_ssd, y_lru, merge, g1, w_a, w_s, w_l, w_o, layer):
    bsz, seq, d = x.shape
    ts = 512
    tok = lambda n: pl.BlockSpec((1, ts, n), lambda b, i: (b, i, 0))
    full = lambda w: pl.BlockSpec((1,) + w.shape[1:], lambda b, i: (layer, 0, 0))
    return pl.pallas_call(
        _merge_kernel,
        grid=(bsz, seq // ts),
        in_specs=[tok(d), tok(y_attn.shape[-1]), tok(y_ssd.shape[-1]), tok(y_lru.shape[-1]), tok(merge.shape[-1]),
                  pl.BlockSpec((1, 1, d), lambda b, i: (b, 0, 0)),
                  full(w_a), full(w_s), full(w_l), full(w_o)],
        out_specs=tok(d),
        out_shape=jax.ShapeDtypeStruct((bsz, seq, d), F32),
        compiler_params=_params("parallel", "parallel"),
        name="merge_block",
    )(x, y_attn, y_ssd, y_lru, merge, g1.reshape(bsz, 1, d), w_a, w_s, w_l, w_o)


ROUTE_E, ROUTE_W, ROUTE_RANK = 0, 2, 4


def _router_kernel(x_ref, g_ref, sc_ref, sh_ref, wr_ref, br_ref, h_ref, route_ref, count_ref, run):
    ts = x_ref.shape[0]

    @pl.when(pl.program_id(0) == 0)
    def _():
        run[...] = jnp.zeros(run.shape, F32)

    x = x_ref[...]
    y = x * lax.rsqrt(jnp.mean(x * x, axis=-1, keepdims=True) + EPS) * g_ref[...]
    h = y * (1.0 + sc_ref[0]) + sh_ref[0]
    h_ref[...] = h
    h_hi = h.astype(BF16)
    r1 = h - h_hi.astype(F32)
    h_mid = r1.astype(BF16)
    h_lo = (r1 - h_mid.astype(F32)).astype(BF16)
    w_hi = wr_ref[0, 0]
    w_lo = wr_ref[0, 1]
    logits = (jnp.dot(h_hi, w_hi, preferred_element_type=F32) + jnp.dot(h_hi, w_lo, preferred_element_type=F32)
              + jnp.dot(h_mid, w_hi, preferred_element_type=F32) + jnp.dot(h_mid, w_lo, preferred_element_type=F32)
              + jnp.dot(h_lo, w_hi, preferred_element_type=F32)) + br_ref[0]
    lane = lax.broadcasted_iota(I32, (ts, LANES), 1)
    big = jnp.int32(LANES)
    lg = jnp.where(lane < MOE_GROUPS, logits, NEG)
    mg = jnp.max(lg, axis=-1, keepdims=True)
    g_sel = jnp.min(jnp.where(lg == mg, lane, big), axis=-1, keepdims=True)
    p_grp = 1.0 / jnp.sum(jnp.exp(lg - mg), axis=-1, keepdims=True)
    lo = MOE_GROUPS + EXPERTS_PER_GROUP * g_sel
    in_grp = (lane >= lo) & (lane < lo + EXPERTS_PER_GROUP)
    le = jnp.where(in_grp, logits, NEG)
    m1 = jnp.max(le, axis=-1, keepdims=True)
    i1 = jnp.min(jnp.where(le == m1, lane, big), axis=-1, keepdims=True)
    den = jnp.sum(jnp.exp(le - m1), axis=-1, keepdims=True)
    le2 = jnp.where(lane == i1, NEG, le)
    m2 = jnp.max(le2, axis=-1, keepdims=True)
    i2 = jnp.min(jnp.where(le2 == m2, lane, big), axis=-1, keepdims=True)
    p1 = 1.0 / den
    p2 = jnp.exp(m2 - m1) / den
    w1 = p_grp * (p1 / (p1 + p2))
    w2 = p_grp * (p2 / (p1 + p2))
    hot1 = lane == i1
    hot2 = lane == i2
    onehot = jnp.where(hot1 | hot2, 1.0, 0.0)
    ri = lax.broadcasted_iota(I32, (ts, ts), 0)
    ci = lax.broadcasted_iota(I32, (ts, ts), 1)
    strict = jnp.where(ri > ci, 1.0, 0.0).astype(BF16)
    before = jnp.dot(strict, onehot.astype(BF16), preferred_element_type=F32) + run[...]
    rank1 = jnp.sum(jnp.where(hot1, before, 0.0), axis=-1, keepdims=True)
    rank2 = jnp.sum(jnp.where(hot2, before, 0.0), axis=-1, keepdims=True)
    run[...] = run[...] + jnp.sum(onehot, axis=0, keepdims=True)
    count_ref[...] = run[...]
    e1 = (i1 - MOE_GROUPS).astype(F32)
    e2 = (i2 - MOE_GROUPS).astype(F32)
    packed = jnp.where(lane == ROUTE_E, e1, 0.0)
    packed = jnp.where(lane == ROUTE_E + 1, e2, packed)
    packed = jnp.where(lane == ROUTE_W, w1, packed)
    packed = jnp.where(lane == ROUTE_W + 1, w2, packed)
    packed = jnp.where(lane == ROUTE_RANK, rank1, packed)
    packed = jnp.where(lane == ROUTE_RANK + 1, rank2, packed)
    route_ref[...] = packed


def router_params(router_wg, router_bg, router_we, router_be):
    depth, d, _ = router_wg.shape
    w_all = jnp.concatenate([router_wg, router_we.reshape(depth, d, N_EXPERTS)], axis=-1)
    w_all = jnp.pad(w_all, ((0, 0), (0, 0), (0, LANES - w_all.shape[-1])))
    w_hi = w_all.astype(BF16)
    w_lo = (w_all - w_hi.astype(F32)).astype(BF16)
    b_all = jnp.concatenate([router_bg, router_be.reshape(depth, N_EXPERTS)], axis=-1)
    b_all = jnp.pad(b_all, ((0, 0), (0, LANES - b_all.shape[-1]))).reshape(depth, 1, LANES)
    return jnp.stack([w_hi, w_lo], axis=1), b_all


def moe_router(x, g, scale, shift, w_stack, b_stack, layer):
    bsz, seq, d = x.shape
    n_tok = bsz * seq
    ts = 512
    per_batch = seq // ts
    return pl.pallas_call(
        _router_kernel,
        grid=(n_tok // ts,),
        in_specs=[pl.BlockSpec((ts, d), lambda i: (i, 0)),
                  pl.BlockSpec((1, d), lambda i: (0, 0)),
                  pl.BlockSpec((1, 1, d), lambda i: (i // per_batch, 0, 0)),
                  pl.BlockSpec((1, 1, d), lambda i: (i // per_batch, 0, 0)),
                  pl.BlockSpec((1, 2, d, LANES), lambda i: (layer, 0, 0, 0)),
                  pl.BlockSpec((1, 1, LANES), lambda i: (layer, 0, 0))],
        out_specs=[pl.BlockSpec((ts, d), lambda i: (i, 0)),
                   pl.BlockSpec((ts, LANES), lambda i: (i, 0)),
                   pl.BlockSpec((1, LANES), lambda i: (0, 0))],
        out_shape=[jax.ShapeDtypeStruct((n_tok, d), F32),
                   jax.ShapeDtypeStruct((n_tok, LANES), F32),
                   jax.ShapeDtypeStruct((1, LANES), F32)],
        scratch_shapes=[pltpu.VMEM((1, LANES), F32)],
        compiler_params=_params("arbitrary"),
        name="moe_router",
    )(x.reshape(n_tok, d), g.reshape(1, d), scale.reshape(bsz, 1, d), shift.reshape(bsz, 1, d), w_stack, b_stack)


DISPATCH_TOKENS = 512


def _dispatch_kernel(d0_ref, d1_ref, h_ref, zero_ref, xd_ref, sem):
    del zero_ref

    def copies(j):
        src = h_ref.at[pl.ds(j, 1), :]
        return (pltpu.make_async_copy(src, xd_ref.at[pl.ds(d0_ref[0, 0, j], 1), :], sem),
                pltpu.make_async_copy(src, xd_ref.at[pl.ds(d1_ref[0, 0, j], 1), :], sem))

    def start(j, carry):
        for cp in copies(j):
            cp.start()
        return carry

    def wait(j, carry):
        for cp in copies(j):
            cp.wait()
        return carry

    lax.fori_loop(0, DISPATCH_TOKENS, start, 0, unroll=8)
    lax.fori_loop(0, DISPATCH_TOKENS, wait, 0, unroll=8)


def moe_dispatch(h, dest, n_rows):
    n_tok, d = h.shape
    tt = DISPATCH_TOKENS
    steps = n_tok // tt
    zeros = jnp.zeros((n_rows, d), h.dtype)
    idx = pl.BlockSpec((1, 1, tt), lambda i: (i, 0, 0), memory_space=pltpu.SMEM)
    return pl.pallas_call(
        _dispatch_kernel,
        grid=(steps,),
        in_specs=[idx, idx,
                  pl.BlockSpec((tt, d), lambda i: (i, 0)),
                  pl.BlockSpec(memory_space=pl.ANY)],
        out_specs=pl.BlockSpec(memory_space=pl.ANY),
        out_shape=jax.ShapeDtypeStruct((n_rows, d), h.dtype),
        input_output_aliases={3: 0},
        scratch_shapes=[pltpu.SemaphoreType.DMA(())],
        compiler_params=pltpu.CompilerParams(dimension_semantics=("arbitrary",), has_side_effects=True),
        name="moe_dispatch",
    )(dest[:, 0].reshape(steps, 1, tt), dest[:, 1].reshape(steps, 1, tt), h, zeros)


def _expert_kernel(be_ref, nb_ref, xd_ref, wg_ref, wu_ref, wd_ref, o_ref, wg_b, wu_b, wd_b):
    i = pl.program_id(0)

    @pl.when(jnp.logical_or(i == 0, be_ref[i] != be_ref[jnp.maximum(i - 1, 0)]))
    def _():
        wg_b[...] = wg_ref[0, 0].astype(BF16)
        wu_b[...] = wu_ref[0, 0].astype(BF16)
        wd_b[...] = wd_ref[0, 0].astype(BF16)

    @pl.when(i < nb_ref[0])
    def _():
        xb = xd_ref[...].astype(BF16)
        gate = jnp.dot(xb, wg_b[...], preferred_element_type=F32)
        up = jnp.dot(xb, wu_b[...], preferred_element_type=F32)
        o_ref[...] = jnp.dot((_silu(gate) * up).astype(BF16), wd_b[...], preferred_element_type=F32)

    @pl.when(i >= nb_ref[0])
    def _():
        o_ref[...] = jnp.zeros(o_ref.shape, F32)


def moe_experts(xd, block_e, n_used, w_gate, w_up, w_down, layer):
    n_rows, d = xd.shape
    ff = w_gate.shape[-1]
    n_blocks = n_rows // MOE_BLOCK
    grid_spec = pltpu.PrefetchScalarGridSpec(
        num_scalar_prefetch=2,
        grid=(n_blocks,),
        in_specs=[pl.BlockSpec((MOE_BLOCK, d), lambda i, be, nb: (i, 0)),
                  pl.BlockSpec((1, 1, d, ff), lambda i, be, nb: (layer, be[i], 0, 0)),
                  pl.BlockSpec((1, 1, d, ff), lambda i, be, nb: (layer, be[i], 0, 0)),
                  pl.BlockSpec((1, 1, ff, d), lambda i, be, nb: (layer, be[i], 0, 0))],
        out_specs=pl.BlockSpec((MOE_BLOCK, d), lambda i, be, nb: (i, 0)),
        scratch_shapes=[pltpu.VMEM((d, ff), BF16), pltpu.VMEM((d, ff), BF16), pltpu.VMEM((ff, d), BF16)],
    )
    return pl.pallas_call(
        _expert_kernel,
        grid_spec=grid_spec,
        out_shape=jax.ShapeDtypeStruct((n_rows, d), F32),
        compiler_params=_params("arbitrary"),
        name="moe_experts",
    )(block_e, n_used, xd, w_gate, w_up, w_down)


COMBINE_TOKENS = 256


def _combine_kernel(d0_ref, d1_ref, x_ref, route_ref, g2_ref, yd_ref, o_ref, rows, sem):
    def copies(j):
        return (pltpu.make_async_copy(yd_ref.at[pl.ds(d0_ref[0, 0, j], 1), :], rows.at[0, pl.ds(j, 1), :], sem),
                pltpu.make_async_copy(yd_ref.at[pl.ds(d1_ref[0, 0, j], 1), :], rows.at[1, pl.ds(j, 1), :], sem))

    def start(j, carry):
        for cp in copies(j):
            cp.start()
        return carry

    def wait(j, carry):
        for cp in copies(j):
            cp.wait()
        return carry

    lax.fori_loop(0, COMBINE_TOKENS, start, 0, unroll=8)
    lax.fori_loop(0, COMBINE_TOKENS, wait, 0, unroll=8)
    route = route_ref[...]
    w1 = route[:, ROUTE_W:ROUTE_W + 1]
    w2 = route[:, ROUTE_W + 1:ROUTE_W + 2]
    y = rows[0] * w1 + rows[1] * w2
    o_ref[...] = x_ref[...] + g2_ref[0] * y


def moe_combine(x, route, g2, yd, dest):
    bsz, seq, d = x.shape
    n_tok = bsz * seq
    tt = COMBINE_TOKENS
    per_batch = seq // tt
    steps = n_tok // tt
    idx = pl.BlockSpec((1, 1, tt), lambda i: (i, 0, 0), memory_space=pltpu.SMEM)
    return pl.pallas_call(
        _combine_kernel,
        grid=(steps,),
        in_specs=[idx, idx,
                  pl.BlockSpec((tt, d), lambda i: (i, 0)),
                  pl.BlockSpec((tt, LANES), lambda i: (i, 0)),
                  pl.BlockSpec((1, 1, d), lambda i: (i // per_batch, 0, 0)),
                  pl.BlockSpec(memory_space=pl.ANY)],
        out_specs=pl.BlockSpec((tt, d), lambda i: (i, 0)),
        out_shape=jax.ShapeDtypeStruct((n_tok, d), F32),
        scratch_shapes=[pltpu.VMEM((TOP_K, tt, d), F32), pltpu.SemaphoreType.DMA(())],
        compiler_params=_params("arbitrary"),
        name="moe_combine",
    )(dest[:, 0].reshape(steps, 1, tt), dest[:, 1].reshape(steps, 1, tt), x.reshape(n_tok, d), route,
      g2.reshape(bsz, 1, d), yd).reshape(bsz, seq, d)


def hier_moe_block(x, g, scale, shift, gate2, router_w, router_b, w_gate, w_up, w_down, layer):
    bsz, seq, d = x.shape
    n_tok = bsz * seq
    n_asg = n_tok * TOP_K
    n_blocks = -(-n_asg // MOE_BLOCK) + N_EXPERTS
    h, route, counts = moe_router(x, g, scale, shift, router_w, router_b, layer)
    counts = counts[0, MOE_GROUPS:MOE_GROUPS + N_EXPERTS].astype(I32)
    pcounts = ((counts + MOE_BLOCK - 1) // MOE_BLOCK) * MOE_BLOCK
    pend = jnp.cumsum(pcounts)
    pstart = pend - pcounts
    expert = route[:, ROUTE_E:ROUTE_E + TOP_K].astype(I32)
    rank = route[:, ROUTE_RANK:ROUTE_RANK + TOP_K].astype(I32)
    experts = jnp.arange(N_EXPERTS, dtype=I32)
    dest = rank + jnp.sum(jnp.where(expert[..., None] == experts, pstart, 0), axis=-1)
    block_rows = jnp.arange(n_blocks, dtype=I32) * MOE_BLOCK
    block_e = jnp.minimum(jnp.sum((pend[None, :] <= block_rows[:, None]).astype(I32), axis=1), N_EXPERTS - 1)
    n_used = (pend[-1:] // MOE_BLOCK).astype(I32)
    xd = moe_dispatch(h, dest, n_blocks * MOE_BLOCK)
    yd = moe_experts(xd, block_e, n_used, w_gate, w_up, w_down, layer)
    return moe_combine(x, route, gate2, yd, dest)


def kernel(x, c, positions, ada_w, ada_b, norm1_g, norm2_g, w_in, ssd_conv_w, ssd_conv_b, ssd_dt_bias, ssd_a_log, ssd_d, ssd_norm_g, lru_conv_w, lru_conv_b, lru_w_r, lru_b_r, lru_w_i, lru_b_i, lru_lambda, w_br_attn, w_br_ssd, w_br_lru, w_out, router_wg, router_bg, router_we, router_be, exp_w_gate, exp_w_up, exp_w_down, final_g):
    depth = ada_w.shape[0]
    bsz, seq, d = x.shape
    n_tok = bsz * seq
    ssd_width = ssd_norm_g.shape[-1]
    xbc_width = ssd_conv_w.shape[-1]
    ssd_heads = ssd_dt_bias.shape[-1]
    lru_width = lru_lambda.shape[-1]
    o_z = 3 * ATTN_WIDTH
    o_xbc = o_z + ssd_width
    o_dt = o_xbc + xbc_width
    o_lru = o_dt + ssd_heads
    o_merge = o_lru + 2 * lru_width
    tn_big, tn_qkv, tn_lru = 1024, ATTN_WIDTH, lru_width
    c_xbc, c_z = 0, xbc_width
    c_merge = c_z + ssd_width
    c_qkv = c_merge + 3 * d
    c_lru = -(-(c_qkv + o_z) // tn_lru) * tn_lru
    c_dt = c_lru + 2 * lru_width
    seg = lambda a, b: w_in[:, :, a:b]
    zeros = lambda n: jnp.zeros(w_in.shape[:2] + (n,), w_in.dtype)
    w_all = jnp.concatenate([seg(o_xbc, o_dt), seg(o_z, o_xbc), seg(o_merge, w_in.shape[-1]),
                             rope_slab_layout(seg(0, 2 * ATTN_WIDTH)), seg(2 * ATTN_WIDTH, o_z),
                             zeros(c_lru - c_qkv - o_z), seg(o_lru, o_merge), seg(o_dt, o_lru),
                             zeros(LANES - ssd_heads)], axis=-1).astype(BF16)
    w_a, w_s, w_l, w_o = (w.astype(BF16) for w in (w_br_attn, w_br_ssd, w_br_lru, w_out))
    lru_w = lru_gate_weights(lru_w_r, lru_w_i)
    router_w, router_b = router_params(router_wg, router_bg, router_we, router_be)

    cond = ada_cond(c, ada_w, ada_b)
    cos, sin = rope_tables(positions)
    for l in range(depth):
        sh1, sc1, g1, sh2, sc2, g2 = [cond[l, :, i * d:(i + 1) * d] for i in range(6)]
        h = norm_modulate(x, norm1_g[l], sc1, sh1, BF16)
        h2d = h.reshape(n_tok, d)
        qkv = matmul_slabs(h, w_all, l, c_qkv, o_z, tn_qkv)
        z = matmul(h2d, w_all, l, c_z, ssd_width, 2048, tn_big)
        xbc = matmul(h2d, w_all, l, c_xbc, xbc_width, 2048, tn_big)
        dt_raw = matmul(h2d, w_all, l, c_dt, LANES, 2048, LANES)
        lru_gx = matmul(h2d, w_all, l, c_lru, 2 * lru_width, 2048, tn_lru)
        merge = matmul(h2d, w_all, l, c_merge, 3 * d, 2048, tn_big)
        y_attn = dilated_attention(qkv, cos, sin)
        y_ssd = ssd_mixer(xbc.reshape(bsz, seq, -1), dt_raw.reshape(bsz, seq, -1), z.reshape(bsz, seq, -1),
                          ssd_conv_w[l], ssd_conv_b[l], ssd_dt_bias[l], ssd_a_log[l], ssd_d[l], ssd_norm_g[l])
        y_lru = rglru_mixer(lru_gx.reshape(bsz, seq, -1), lru_conv_w[l], lru_conv_b[l], lru_w, lru_b_r[l],
                            lru_b_i[l], lru_lambda[l], l)
        x = merge_block(x, y_attn, y_ssd, y_lru, merge.reshape(bsz, seq, -1), g1, w_a, w_s, w_l, w_o, l)
        x = hier_moe_block(x, norm2_g[l], sc2, sh2, g2, router_w, router_b, exp_w_gate, exp_w_up, exp_w_down, l)
    return final_norm(x, final_g)
```

```python
import functools
import math

import jax
import jax.numpy as jnp
import numpy as np
from jax import lax
from jax.experimental import pallas as pl
from jax.experimental.pallas import tpu as pltpu

F32 = jnp.float32
BF16 = jnp.bfloat16
I32 = jnp.int32

LANES = 128
SUBLANES = 8
VMEM_BYTES = 64 << 20
VMEM_LIMIT = (VMEM_BYTES * 3) // 4

EPS = 1e-6
HEAD_DIM = 64
ATTN_DILATIONS = (1, 4, 16)
ATTN_BLK = 128
ATTN_UNROLL = 4
HEADS_PER_GROUP = 8
N_ATTN_GROUPS = 3
ATTN_WIDTH = N_ATTN_GROUPS * HEADS_PER_GROUP * HEAD_DIM
ROPE_THETA = 10000.0
SSD_HEAD_DIM = 64
SSD_GROUPS = 8
SSD_HEADS_PER_GROUP = 4
SSD_STATE = 128
SSD_CHUNK = 128
CONV_K = 4
LRU_BLOCKS = 16
LRU_BLOCK_WIDTH = 80
LRU_C = 8.0
MOE_GROUPS = 4
EXPERTS_PER_GROUP = 8
N_EXPERTS = MOE_GROUPS * EXPERTS_PER_GROUP
TOP_K = 2
MOE_BLOCK = 512
NEG = -1e30


def _params(*sem):
    return pltpu.CompilerParams(dimension_semantics=sem, vmem_limit_bytes=VMEM_LIMIT)


def _sigmoid(x):
    return 0.5 * jnp.tanh(0.5 * x) + 0.5


def _silu(x):
    return x * _sigmoid(x)


def _softplus(x):
    return jnp.maximum(x, 0.0) + jnp.log1p(jnp.exp(-jnp.abs(x)))


def _ada_kernel(c_ref, w_ref, b_ref, o_ref):
    c_act = _silu(c_ref[...]).astype(BF16)
    o_ref[0] = jnp.dot(c_act, w_ref[0].astype(BF16), preferred_element_type=F32) + b_ref[0]


def ada_cond(c, ada_w, ada_b):
    depth, d_model, n = ada_w.shape
    bsz = c.shape[0]
    tn = 2048
    return pl.pallas_call(
        _ada_kernel,
        grid=(depth, n // tn),
        in_specs=[pl.BlockSpec((bsz, d_model), lambda l, j: (0, 0)),
                  pl.BlockSpec((1, d_model, tn), lambda l, j: (l, 0, j)),
                  pl.BlockSpec((1, 1, tn), lambda l, j: (l, 0, j))],
        out_specs=pl.BlockSpec((1, bsz, tn), lambda l, j: (l, 0, j)),
        out_shape=jax.ShapeDtypeStruct((depth, bsz, n), F32),
        compiler_params=_params("parallel", "parallel"),
        name="ada_cond",
    )(c, ada_w, ada_b.reshape(depth, 1, n))


def _norm_kernel(x_ref, g_ref, sc_ref, sh_ref, o_ref):
    x = x_ref[0]
    y = x * lax.rsqrt(jnp.mean(x * x, axis=-1, keepdims=True) + EPS) * g_ref[...]
    o_ref[0] = (y * (1.0 + sc_ref[0]) + sh_ref[0]).astype(o_ref.dtype)


def _plain_norm_kernel(x_ref, g_ref, o_ref):
    x = x_ref[0]
    o_ref[0] = x * lax.rsqrt(jnp.mean(x * x, axis=-1, keepdims=True) + EPS) * g_ref[...]


def norm_modulate(x, g, scale, shift, out_dtype):
    bsz, seq, d = x.shape
    ts = 512
    return pl.pallas_call(
        _norm_kernel,
        grid=(bsz, seq // ts),
        in_specs=[pl.BlockSpec((1, ts, d), lambda b, i: (b, i, 0)),
                  pl.BlockSpec((1, d), lambda b, i: (0, 0)),
                  pl.BlockSpec((1, 1, d), lambda b, i: (b, 0, 0)),
                  pl.BlockSpec((1, 1, d), lambda b, i: (b, 0, 0))],
        out_specs=pl.BlockSpec((1, ts, d), lambda b, i: (b, i, 0)),
        out_shape=jax.ShapeDtypeStruct((bsz, seq, d), out_dtype),
        compiler_params=_params("parallel", "parallel"),
        name="norm_modulate",
    )(x, g.reshape(1, d), scale.reshape(bsz, 1, d), shift.reshape(bsz, 1, d))


def final_norm(x, g):
    bsz, seq, d = x.shape
    ts = 512
    return pl.pallas_call(
        _plain_norm_kernel,
        grid=(bsz, seq // ts),
        in_specs=[pl.BlockSpec((1, ts, d), lambda b, i: (b, i, 0)),
                  pl.BlockSpec((1, d), lambda b, i: (0, 0))],
        out_specs=pl.BlockSpec((1, ts, d), lambda b, i: (b, i, 0)),
        out_shape=jax.ShapeDtypeStruct((bsz, seq, d), F32),
        compiler_params=_params("parallel", "parallel"),
        name="final_norm",
    )(x, g.reshape(1, d))


def _mm_kernel(a_ref, w_ref, o_ref):
    o_ref[...] = jnp.dot(a_ref[...], w_ref[0], preferred_element_type=F32).astype(o_ref.dtype)


def matmul(a, w_stack, layer, col0, n, tm, tn, out_dtype=F32):
    m, k = a.shape
    j0 = col0 // tn
    assert col0 % tn == 0 and n % tn == 0 and m % tm == 0
    return pl.pallas_call(
        _mm_kernel,
        grid=(m // tm, n // tn),
        in_specs=[pl.BlockSpec((tm, k), lambda i, j: (i, 0)),
                  pl.BlockSpec((1, k, tn), lambda i, j: (layer, 0, j0 + j))],
        out_specs=pl.BlockSpec((tm, tn), lambda i, j: (i, j)),
        out_shape=jax.ShapeDtypeStruct((m, n), out_dtype),
        compiler_params=_params("parallel", "parallel"),
        name="matmul",
    )(a, w_stack)


def _mm_slab_kernel(a_ref, w_ref, o_ref):
    res = jnp.dot(a_ref[0], w_ref[0], preferred_element_type=F32).astype(o_ref.dtype)
    for s in range(o_ref.shape[1]):
        o_ref[0, s] = res[:, s * LANES:(s + 1) * LANES]


def matmul_slabs(a, w_stack, layer, col0, n, tn):
    bsz, seq, k = a.shape
    j0 = col0 // tn
    assert col0 % tn == 0 and n % tn == 0
    return pl.pallas_call(
        _mm_slab_kernel,
        grid=(bsz, n // tn),
        in_specs=[pl.BlockSpec((1, seq, k), lambda b, j: (b, 0, 0)),
                  pl.BlockSpec((1, k, tn), lambda b, j: (layer, 0, j0 + j))],
        out_specs=pl.BlockSpec((1, tn // LANES, seq, LANES), lambda b, j: (b, j, 0, 0)),
        out_shape=jax.ShapeDtypeStruct((bsz, n // LANES, seq, LANES), BF16),
        compiler_params=_params("parallel", "parallel"),
        name="matmul_slabs",
    )(a, w_stack)


def _rope_table_kernel(pos_ref, freq_ref, sign_ref, cos_ref, sin_ref):
    ang = pos_ref[0] * freq_ref[...]
    cos_ref[0] = jnp.cos(ang)
    sin_ref[0] = jnp.sin(ang) * sign_ref[...]


def rope_slab_layout(w):
    half = HEAD_DIM // 2
    lead = w.shape[:-1]
    w = w.reshape(lead + (w.shape[-1] // LANES, 2, 2, half))
    return jnp.swapaxes(w, -2, -3).reshape(lead + (-1,))


def rope_tables(positions):
    bsz, seq = positions.shape
    half = HEAD_DIM // 2
    freqs = ROPE_THETA ** (-jnp.arange(half, dtype=F32) / half)
    freq_row = jnp.tile(freqs, LANES // half).reshape(1, LANES)
    sign_row = jnp.concatenate([-jnp.ones((LANES // 2,), F32), jnp.ones((LANES // 2,), F32)])
    pos = jnp.broadcast_to(positions.astype(F32)[..., None], (bsz, seq, LANES))
    ts = 512
    spec = pl.BlockSpec((1, ts, LANES), lambda b, i: (b, i, 0))
    row = pl.BlockSpec((1, LANES), lambda b, i: (0, 0))
    return pl.pallas_call(
        _rope_table_kernel,
        grid=(bsz, seq // ts),
        in_specs=[spec, row, row],
        out_specs=[spec, spec],
        out_shape=[jax.ShapeDtypeStruct((bsz, seq, LANES), F32)] * 2,
        compiler_params=_params("parallel", "parallel"),
        name="rope_tables",
    )(pos, freq_row, sign_row.reshape(1, LANES))


def _attn_kernel(q1, k1, v1, q2, k2, v2, q3, k3, v3, cos_ref, sin_ref, o_ref, qr, kr, vr, og, lg):
    seq = o_ref.shape[1]
    blk = ATTN_BLK
    pad = kr.shape[0] - seq
    rope_rows = 256
    lane_b = lax.broadcasted_iota(I32, (blk, LANES), 1)
    head0 = lane_b < HEAD_DIM
    qk_head0 = (lane_b % HEAD_DIM) < (HEAD_DIM // 2)
    qi = lax.broadcasted_iota(I32, (2 * blk, 2 * blk), 0) % blk
    ki = lax.broadcasted_iota(I32, (2 * blk, 2 * blk), 1)
    bias_prev = jnp.where(ki >= qi, 0.0, NEG).astype(F32)
    bias_cur = jnp.where(ki - blk <= qi, 0.0, NEG).astype(F32)
    bias_inner = jnp.where(ki < blk, bias_prev, bias_cur)
    bias_first = jnp.where(ki < blk, NEG, bias_cur)
    nt = (((1,), (1,)), ((), ()))
    ones = jnp.ones((2 * blk, LANES), BF16)
    kr[0:pad, :] = jnp.zeros((pad, LANES), F32)
    vr[0:pad, :] = jnp.zeros((pad, LANES), F32)

    for g, (q_ref, k_ref, v_ref) in enumerate(((q1, k1, v1), (q2, k2, v2), (q3, k3, v3))):
        dil = ATTN_DILATIONS[g]
        n_blk = seq // (dil * blk)

        def rope_body(i, carry, q_ref=q_ref, k_ref=k_ref, v_ref=v_ref):
            rows = pl.ds(pl.multiple_of(i * rope_rows, rope_rows), rope_rows)
            rows_p = pl.ds(pl.multiple_of(pad + i * rope_rows, rope_rows), rope_rows)
            cos = cos_ref[0, rows, :]
            sin = sin_ref[0, rows, :]
            for src, scale in ((q_ref, HEAD_DIM ** -0.5), (k_ref, None)):
                t = src[0, 0, rows, :].astype(F32)
                r = t * cos + pltpu.roll(t, LANES // 2, 1) * sin
                if scale is None:
                    kr[rows_p, :] = r
                else:
                    qr[rows, :] = r * scale
            vr[rows_p, :] = v_ref[0, 0, rows, :].astype(F32)
            return carry

        lax.fori_loop(0, seq // rope_rows, rope_body, 0)

        def blk_body(it, carry, dil=dil, n_blk=n_blk, g=g):
            blocks = []
            for u in range(ATTN_UNROLL):
                idx = it * ATTN_UNROLL + u
                phase = idx // n_blk
                n = idx % n_blk
                start = phase + dil * blk * n
                cur = pl.ds(start, blk, stride=dil) if dil > 1 else pl.ds(start, blk)
                both = (pl.ds(pad + start - dil * blk, 2 * blk, stride=dil) if dil > 1
                        else pl.ds(pad + start - blk, 2 * blk))
                blocks.append((cur, both, n))
            scores = []
            for cur, both, n in blocks:
                q = qr[cur, :]
                q2 = jnp.concatenate([jnp.where(qk_head0, q, 0.0), jnp.where(qk_head0, 0.0, q)], axis=0).astype(BF16)
                s = lax.dot_general(q2, kr[both, :].astype(BF16), nt, preferred_element_type=F32)
                scores.append(s + jnp.where(n > 0, bias_inner, bias_first))
            probs = []
            for s in scores:
                m = jnp.max(jnp.maximum(s[:, :blk], s[:, blk:]), axis=-1, keepdims=True)
                probs.append((m, jnp.exp(s - m).astype(BF16)))
            for (cur, both, _), (m, p) in zip(blocks, probs):
                v1 = jnp.concatenate([vr[both, :].astype(BF16), ones], axis=1)
                r = jnp.dot(p, v1, preferred_element_type=F32)
                den = jnp.where(head0, r[:blk, LANES:], r[blk:, LANES:])
                og[g, cur, :] = jnp.where(head0, r[:blk, :LANES], r[blk:, :LANES]) / den
                m_b = jnp.broadcast_to(m, (2 * blk, LANES))
                lg[g, cur, :] = jnp.where(head0, m_b[:blk], m_b[blk:]) + jnp.log(den)
            return carry

        lax.fori_loop(0, dil * n_blk // ATTN_UNROLL, blk_body, 0)

    def mix_body(i, carry):
        rows = pl.ds(pl.multiple_of(i * rope_rows, rope_rows), rope_rows)
        l0, l1, l2 = lg[0, rows, :], lg[1, rows, :], lg[2, rows, :]
        m = jnp.maximum(jnp.maximum(l0, l1), l2)
        e0, e1, e2 = jnp.exp(l0 - m), jnp.exp(l1 - m), jnp.exp(l2 - m)
        tot = e0 + e1 + e2
        y = (e0 / tot) * og[0, rows, :] + (e1 / tot) * og[1, rows, :] + (e2 / tot) * og[2, rows, :]
        o_ref[0, rows, :] = y.astype(o_ref.dtype)
        return carry

    lax.fori_loop(0, seq // rope_rows, mix_body, 0)


def dilated_attention(qkv, cos, sin):
    bsz, n_slabs, seq, _ = qkv.shape
    pairs = HEADS_PER_GROUP * HEAD_DIM // LANES
    per_proj = n_slabs // 3
    pad = ATTN_BLK * max(ATTN_DILATIONS)

    def slab(proj, g):
        return pl.BlockSpec((1, 1, seq, LANES), lambda b, j: (b, proj * per_proj + g * pairs + j, 0, 0))

    in_specs = []
    for g in range(N_ATTN_GROUPS):
        in_specs += [slab(0, g), slab(1, g), slab(2, g)]
    table = pl.BlockSpec((1, seq, LANES), lambda b, j: (b, 0, 0))
    in_specs += [table, table]
    return pl.pallas_call(
        _attn_kernel,
        grid=(bsz, pairs),
        in_specs=in_specs,
        out_specs=pl.BlockSpec((1, seq, LANES), lambda b, j: (b, 0, j)),
        out_shape=jax.ShapeDtypeStruct((bsz, seq, pairs * LANES), BF16),
        scratch_shapes=[pltpu.VMEM((seq, LANES), F32), pltpu.VMEM((pad + seq, LANES), F32),
                        pltpu.VMEM((pad + seq, LANES), F32),
                        pltpu.VMEM((N_ATTN_GROUPS, seq, LANES), F32), pltpu.VMEM((N_ATTN_GROUPS, seq, LANES), F32)],
        compiler_params=_params("parallel", "parallel"),
        name="dilated_attention",
    )(*([qkv] * 9), cos, sin)


def _causal_conv(buf, x, w_ref, b_ref, rows):
    buf[SUBLANES:SUBLANES + rows, :] = x
    acc = b_ref[...] + w_ref[CONV_K - 1:CONV_K, :] * x
    for k in range(CONV_K - 1):
        off = SUBLANES - (CONV_K - 1) + k
        acc = acc + w_ref[k:k + 1, :] * buf[off:off + rows, :]
    buf[0:SUBLANES, :] = buf[rows:rows + SUBLANES, :]
    return acc


def _by_head(cols, lane, width):
    out = cols[-1]
    for j in range(len(cols) - 2, -1, -1):
        out = jnp.where(lane < (j + 1) * width, cols[j], out)
    return out


def _ssd_kernel(xbc_ref, dt_ref, z_ref, cw_ref, cb_ref, dtb_ref, a_ref, dskip_ref, ng_ref, o_ref,
                buf, state, ybuf):
    chunk = SSD_CHUNK
    width = SSD_GROUPS * SSD_HEADS_PER_GROUP * SSD_HEAD_DIM
    gw = SSD_HEADS_PER_GROUP * SSD_HEAD_DIM

    @pl.when(pl.program_id(1) == 0)
    def _():
        buf[0:SUBLANES, :] = jnp.zeros((SUBLANES, buf.shape[1]), F32)
        state[...] = jnp.zeros(state.shape, F32)

    xbc = _silu(_causal_conv(buf, xbc_ref[0].astype(F32), cw_ref, cb_ref, chunk))
    dt = _softplus(dt_ref[0] + dtb_ref[...])
    a = dt * a_ref[...]
    li = lax.broadcasted_iota(I32, (chunk, chunk), 0)
    si = lax.broadcasted_iota(I32, (chunk, chunk), 1)
    tril = li >= si
    tri = jnp.where(tril, 1.0, 0.0).astype(BF16)
    a_hi = a.astype(BF16)
    r1 = a - a_hi.astype(F32)
    a_mid = r1.astype(BF16)
    a_lo = (r1 - a_mid.astype(F32)).astype(BF16)
    a_cs = (jnp.dot(tri, a_hi, preferred_element_type=F32) + jnp.dot(tri, a_mid, preferred_element_type=F32)
            + jnp.dot(tri, a_lo, preferred_element_type=F32))
    a_cs_t = a_cs.T
    a_last = a_cs[chunk - 1:chunk, :]
    e_acs = jnp.exp(a_cs)
    e_dec = jnp.exp(a_last - a_cs)
    e_last = jnp.exp(a_last)
    lane_g = lax.broadcasted_iota(I32, (chunk, gw), 1)
    lane_1 = lax.broadcasted_iota(I32, (1, gw), 1)

    for g in range(SSD_GROUPS):
        heads = range(g * SSD_HEADS_PER_GROUP, (g + 1) * SSD_HEADS_PER_GROUP)
        xs = xbc[:, g * gw:(g + 1) * gw]
        bm = xbc[:, width + g * SSD_STATE: width + (g + 1) * SSD_STATE]
        cm = xbc[:, width + SSD_GROUPS * SSD_STATE + g * SSD_STATE: width + SSD_GROUPS * SSD_STATE + (g + 1) * SSD_STATE]
        bm_t = bm.T.astype(BF16)
        cm_b = cm.astype(BF16)
        cb = jnp.dot(cm_b, bm_t, preferred_element_type=F32)
        dt_g = _by_head([dt[:, h:h + 1] for h in heads], lane_g, SSD_HEAD_DIM)
        x_dt = xs * dt_g
        x_dt_b = x_dt.astype(BF16)
        diag = []
        for h in heads:
            l_mat = jnp.where(tril, jnp.exp(a_cs[:, h:h + 1] - a_cs_t[h:h + 1, :]), 0.0)
            diag.append(jnp.dot((cb * l_mat).astype(BF16), x_dt_b, preferred_element_type=F32))
        y = _by_head(diag, lane_g, SSD_HEAD_DIM)
        s_prev = state[g]
        y = y + jnp.dot(cm_b, s_prev.astype(BF16), preferred_element_type=F32) * _by_head(
            [e_acs[:, h:h + 1] for h in heads], lane_g, SSD_HEAD_DIM)
        x_dec = (x_dt * _by_head([e_dec[:, h:h + 1] for h in heads], lane_g, SSD_HEAD_DIM)).astype(BF16)
        state[g] = (s_prev * _by_head([e_last[:, h:h + 1] for h in heads], lane_1, SSD_HEAD_DIM)
                    + jnp.dot(bm_t, x_dec, preferred_element_type=F32))
        ybuf[:, g * gw:(g + 1) * gw] = y + dskip_ref[:, g * gw:(g + 1) * gw] * xs

    v = ybuf[...] * _silu(z_ref[0].astype(F32))
    o_ref[0] = (v * lax.rsqrt(jnp.mean(v * v, axis=-1, keepdims=True) + EPS) * ng_ref[...]).astype(o_ref.dtype)


def ssd_mixer(xbc, dt_raw, z, conv_w, conv_b, dt_bias, a_log, d_skip, norm_g):
    bsz, seq, cw = xbc.shape
    width = z.shape[-1]
    heads = dt_bias.shape[0]
    pad = LANES - heads
    dtb = jnp.pad(dt_bias, (0, pad)).reshape(1, LANES)
    a_neg = jnp.pad(-jnp.exp(a_log), (0, pad)).reshape(1, LANES)
    dskip = jnp.repeat(d_skip, SSD_HEAD_DIM).reshape(1, width)
    row = lambda n: pl.BlockSpec((1, n), lambda b, c: (0, 0))
    return pl.pallas_call(
        _ssd_kernel,
        grid=(bsz, seq // SSD_CHUNK),
        in_specs=[pl.BlockSpec((1, SSD_CHUNK, cw), lambda b, c: (b, c, 0)),
                  pl.BlockSpec((1, SSD_CHUNK, LANES), lambda b, c: (b, c, 0)),
                  pl.BlockSpec((1, SSD_CHUNK, width), lambda b, c: (b, c, 0)),
                  pl.BlockSpec((CONV_K, cw), lambda b, c: (0, 0)),
                  row(cw), row(LANES), row(LANES), row(width), row(width)],
        out_specs=pl.BlockSpec((1, SSD_CHUNK, width), lambda b, c: (b, c, 0)),
        out_shape=jax.ShapeDtypeStruct((bsz, seq, width), BF16),
        scratch_shapes=[pltpu.VMEM((SSD_CHUNK + SUBLANES, cw), F32),
                        pltpu.VMEM((SSD_GROUPS, SSD_STATE, SSD_HEADS_PER_GROUP * SSD_HEAD_DIM), F32),
                        pltpu.VMEM((SSD_CHUNK, width), F32)],
        compiler_params=_params("parallel", "arbitrary"),
        name="ssd_mixer",
    )(xbc, dt_raw, z, conv_w, conv_b.reshape(1, cw), dtb, a_neg, dskip, norm_g.reshape(1, width))


def _lru_kernel(x_ref, gate_ref, cw_ref, cb_ref, w_ref, br_ref, bi_ref, lam_ref, o_ref, buf, a_s, u_s, h_s):
    rows = x_ref.shape[1]
    width = x_ref.shape[2]
    half = width // 2

    @pl.when(pl.program_id(1) == 0)
    def _():
        buf[0:SUBLANES, :] = jnp.zeros((SUBLANES, width), F32)
        h_s[...] = jnp.zeros(h_s.shape, F32)

    xc = _causal_conv(buf, x_ref[0].astype(F32), cw_ref, cb_ref, rows)
    sp = _softplus(-lam_ref[...])
    for c in range(2):
        cols = slice(c * half, (c + 1) * half)
        pre = jnp.dot(xc[:, cols].astype(BF16), w_ref[0, c], preferred_element_type=F32)
        r_gate = _sigmoid(pre[:, :half] + br_ref[:, cols])
        i_gate = _sigmoid(pre[:, half:] + bi_ref[:, cols])
        log_a = -LRU_C * r_gate * sp[:, cols]
        t = jnp.tanh(log_a)
        one_minus_a2 = -2.0 * t / (1.0 - t)
        a_s[:, cols] = jnp.exp(log_a)
        u_s[:, cols] = jnp.sqrt(one_minus_a2) * (i_gate * xc[:, cols])

    sub = lax.broadcasted_iota(I32, (SUBLANES, width), 0)

    def scan_body(i, h):
        tile = pl.ds(pl.multiple_of(i * SUBLANES, SUBLANES), SUBLANES)
        a = a_s[tile, :]
        u = u_s[tile, :]
        for s in (1, 2, 4):
            keep = sub >= s
            a_sh = jnp.where(keep, pltpu.roll(a, s, 0), 1.0)
            u_sh = jnp.where(keep, pltpu.roll(u, s, 0), 0.0)
            u = a * u_sh + u
            a = a * a_sh
        hs = u + a * h
        u_s[tile, :] = hs
        return jnp.broadcast_to(hs[SUBLANES - 1:SUBLANES, :], (SUBLANES, width))

    h_s[...] = lax.fori_loop(0, rows // SUBLANES, scan_body, h_s[...])
    o_ref[0] = (u_s[...] * jax.nn.gelu(gate_ref[0].astype(F32), approximate=True)).astype(o_ref.dtype)


def lru_gate_weights(w_r, w_i):
    depth = w_r.shape[0]
    per_half = LRU_BLOCKS // 2
    half = per_half * LRU_BLOCK_WIDTH

    def block_diag(w):
        eye = jnp.eye(per_half, dtype=w.dtype)
        wb = w.reshape(depth, 2, per_half, LRU_BLOCK_WIDTH, LRU_BLOCK_WIDTH)
        return jnp.einsum('dckij,kl->dckilj', wb, eye).reshape(depth, 2, half, half)

    return jnp.concatenate([block_diag(w_r), block_diag(w_i)], axis=-1).astype(BF16)


def rglru_mixer(lru_gx, conv_w, conv_b, w_cat, b_r, b_i, lam, layer):
    bsz, seq, w2 = lru_gx.shape
    width = w2 // 2
    half = width // 2
    ts = 256
    row = pl.BlockSpec((1, width), lambda b, i: (0, 0))
    return pl.pallas_call(
        _lru_kernel,
        grid=(bsz, seq // ts),
        in_specs=[pl.BlockSpec((1, ts, width), lambda b, i: (b, i, 1)),
                  pl.BlockSpec((1, ts, width), lambda b, i: (b, i, 0)),
                  pl.BlockSpec((CONV_K, width), lambda b, i: (0, 0)),
                  row,
                  pl.BlockSpec((1, 2, half, width), lambda b, i: (layer, 0, 0, 0)),
                  row, row, row],
        out_specs=pl.BlockSpec((1, ts, width), lambda b, i: (b, i, 0)),
        out_shape=jax.ShapeDtypeStruct((bsz, seq, width), BF16),
        scratch_shapes=[pltpu.VMEM((ts + SUBLANES, width), F32), pltpu.VMEM((ts, width), F32),
                        pltpu.VMEM((ts, width), F32), pltpu.VMEM((SUBLANES, width), F32)],
        compiler_params=_params("parallel", "arbitrary"),
        name="rglru_mixer",
    )(lru_gx, lru_gx, conv_w, conv_b.reshape(1, width), w_cat, b_r.reshape(1, width), b_i.reshape(1, width),
      lam.reshape(1, width))


def _merge_kernel(x_ref, ya_ref, ys_ref, yl_ref, mg_ref, g1_ref, wa_ref, ws_ref, wl_ref, wo_ref, o_ref):
    d = x_ref.shape[2]
    gates = _sigmoid(mg_ref[0].astype(F32))
    merged = (gates[:, :d] * jnp.dot(ya_ref[0], wa_ref[0], preferred_element_type=F32)
              + gates[:, d:2 * d] * jnp.dot(ys_ref[0], ws_ref[0], preferred_element_type=F32)
              + gates[:, 2 * d:] * jnp.dot(yl_ref[0], wl_ref[0], preferred_element_type=F32))
    y = jnp.dot(merged.astype(BF16), wo_ref[0], preferred_element_type=F32)
    o_ref[0] = x_ref[0] + g1_ref[0] * y


def merge_block(x, y_attn, y_ssd, y_lru, merge, g1, w_a, w_s, w_l, w_o, layer):
    bsz, seq, d = x.shape
    ts = 512
    tok = lambda n: pl.BlockSpec((1, ts, n), lambda b, i: (b, i, 0))
    full = lambda w: pl.BlockSpec((1,) + w.shape[1:], lambda b, i: (layer, 0, 0))
    return pl.pallas_call(
        _merge_kernel,
        grid=(bsz, seq // ts),
        in_specs=[tok(d), tok(y_attn.shape[-1]), tok(y_ssd.shape[-1]), tok(y_lru.shape[-1]), tok(merge.shape[-1]),
                  pl.BlockSpec((1, 1, d), lambda b, i: (b, 0, 0)),
                  full(w_a), full(w_s), full(w_l), full(w_o)],
        out_specs=tok(d),
        out_shape=jax.ShapeDtypeStruct((bsz, seq, d), F32),
        compiler_params=_params("parallel", "parallel"),
        name="merge_block",
    )(x, y_attn, y_ssd, y_lru, merge, g1.reshape(bsz, 1, d), w_a, w_s, w_l, w_o)


ROUTE_E, ROUTE_W, ROUTE_RANK = 0, 2, 4


def _router_kernel(x_ref, g_ref, sc_ref, sh_ref, wr_ref, br_ref, h_ref, route_ref, count_ref, run):
    ts = x_ref.shape[0]

    @pl.when(pl.program_id(0) == 0)
    def _():
        run[...] = jnp.zeros(run.shape, F32)

    x = x_ref[...]
    y = x * lax.rsqrt(jnp.mean(x * x, axis=-1, keepdims=True) + EPS) * g_ref[...]
    h = y * (1.0 + sc_ref[0]) + sh_ref[0]
    h_ref[...] = h
    h_hi = h.astype(BF16)
    r1 = h - h_hi.astype(F32)
    h_mid = r1.astype(BF16)
    h_lo = (r1 - h_mid.astype(F32)).astype(BF16)
    w_hi = wr_ref[0, 0]
    w_lo = wr_ref[0, 1]
    logits = (jnp.dot(h_hi, w_hi, preferred_element_type=F32) + jnp.dot(h_hi, w_lo, preferred_element_type=F32)
              + jnp.dot(h_mid, w_hi, preferred_element_type=F32) + jnp.dot(h_mid, w_lo, preferred_element_type=F32)
              + jnp.dot(h_lo, w_hi, preferred_element_type=F32)) + br_ref[0]
    lane = lax.broadcasted_iota(I32, (ts, LANES), 1)
    big = jnp.int32(LANES)
    lg = jnp.where(lane < MOE_GROUPS, logits, NEG)
    mg = jnp.max(lg, axis=-1, keepdims=True)
    g_sel = jnp.min(jnp.where(lg == mg, lane, big), axis=-1, keepdims=True)
    p_grp = 1.0 / jnp.sum(jnp.exp(lg - mg), axis=-1, keepdims=True)
    lo = MOE_GROUPS + EXPERTS_PER_GROUP * g_sel
    in_grp = (lane >= lo) & (lane < lo + EXPERTS_PER_GROUP)
    le = jnp.where(in_grp, logits, NEG)
    m1 = jnp.max(le, axis=-1, keepdims=True)
    i1 = jnp.min(jnp.where(le == m1, lane, big), axis=-1, keepdims=True)
    den = jnp.sum(jnp.exp(le - m1), axis=-1, keepdims=True)
    le2 = jnp.where(lane == i1, NEG, le)
    m2 = jnp.max(le2, axis=-1, keepdims=True)
    i2 = jnp.min(jnp.where(le2 == m2, lane, big), axis=-1, keepdims=True)
    p1 = 1.0 / den
    p2 = jnp.exp(m2 - m1) / den
    w1 = p_grp * (p1 / (p1 + p2))
    w2 = p_grp * (p2 / (p1 + p2))
    hot1 = lane == i1
    hot2 = lane == i2
    onehot = jnp.where(hot1 | hot2, 1.0, 0.0)
    ri = lax.broadcasted_iota(I32, (ts, ts), 0)
    ci = lax.broadcasted_iota(I32, (ts, ts), 1)
    strict = jnp.where(ri > ci, 1.0, 0.0).astype(BF16)
    before = jnp.dot(strict, onehot.astype(BF16), preferred_element_type=F32) + run[...]
    rank1 = jnp.sum(jnp.where(hot1, before, 0.0), axis=-1, keepdims=True)
    rank2 = jnp.sum(jnp.where(hot2, before, 0.0), axis=-1, keepdims=True)
    run[...] = run[...] + jnp.sum(onehot, axis=0, keepdims=True)
    count_ref[...] = run[...]
    e1 = (i1 - MOE_GROUPS).astype(F32)
    e2 = (i2 - MOE_GROUPS).astype(F32)
    packed = jnp.where(lane == ROUTE_E, e1, 0.0)
    packed = jnp.where(lane == ROUTE_E + 1, e2, packed)
    packed = jnp.where(lane == ROUTE_W, w1, packed)
    packed = jnp.where(lane == ROUTE_W + 1, w2, packed)
    packed = jnp.where(lane == ROUTE_RANK, rank1, packed)
    packed = jnp.where(lane == ROUTE_RANK + 1, rank2, packed)
    route_ref[...] = packed


def router_params(router_wg, router_bg, router_we, router_be):
    depth, d, _ = router_wg.shape
    w_all = jnp.concatenate([router_wg, router_we.reshape(depth, d, N_EXPERTS)], axis=-1)
    w_all = jnp.pad(w_all, ((0, 0), (0, 0), (0, LANES - w_all.shape[-1])))
    w_hi = w_all.astype(BF16)
    w_lo = (w_all - w_hi.astype(F32)).astype(BF16)
    b_all = jnp.concatenate([router_bg, router_be.reshape(depth, N_EXPERTS)], axis=-1)
    b_all = jnp.pad(b_all, ((0, 0), (0, LANES - b_all.shape[-1]))).reshape(depth, 1, LANES)
    return jnp.stack([w_hi, w_lo], axis=1), b_all


def moe_router(x, g, scale, shift, w_stack, b_stack, layer):
    bsz, seq, d = x.shape
    n_tok = bsz * seq
    ts = 512
    per_batch = seq // ts
    return pl.pallas_call(
        _router_kernel,
        grid=(n_tok // ts,),
        in_specs=[pl.BlockSpec((ts, d), lambda i: (i, 0)),
                  pl.BlockSpec((1, d), lambda i: (0, 0)),
                  pl.BlockSpec((1, 1, d), lambda i: (i // per_batch, 0, 0)),
                  pl.BlockSpec((1, 1, d), lambda i: (i // per_batch, 0, 0)),
                  pl.BlockSpec((1, 2, d, LANES), lambda i: (layer, 0, 0, 0)),
                  pl.BlockSpec((1, 1, LANES), lambda i: (layer, 0, 0))],
        out_specs=[pl.BlockSpec((ts, d), lambda i: (i, 0)),
                   pl.BlockSpec((ts, LANES), lambda i: (i, 0)),
                   pl.BlockSpec((1, LANES), lambda i: (0, 0))],
        out_shape=[jax.ShapeDtypeStruct((n_tok, d), F32),
                   jax.ShapeDtypeStruct((n_tok, LANES), F32),
                   jax.ShapeDtypeStruct((1, LANES), F32)],
        scratch_shapes=[pltpu.VMEM((1, LANES), F32)],
        compiler_params=_params("arbitrary"),
        name="moe_router",
    )(x.reshape(n_tok, d), g.reshape(1, d), scale.reshape(bsz, 1, d), shift.reshape(bsz, 1, d), w_stack, b_stack)


DISPATCH_TOKENS = 1024


def _dispatch_kernel(d0_ref, d1_ref, h_ref, zero_ref, xd_ref, sem):
    del zero_ref

    def copies(j):
        src = h_ref.at[pl.ds(j, 1), :]
        return (pltpu.make_async_copy(src, xd_ref.at[pl.ds(d0_ref[0, 0, j], 1), :], sem),
                pltpu.make_async_copy(src, xd_ref.at[pl.ds(d1_ref[0, 0, j], 1), :], sem))

    def start(j, carry):
        for cp in copies(j):
            cp.start()
        return carry

    def wait(j, carry):
        for cp in copies(j):
            cp.wait()
        return carry

    lax.fori_loop(0, DISPATCH_TOKENS, start, 0, unroll=8)
    lax.fori_loop(0, DISPATCH_TOKENS, wait, 0, unroll=8)


def moe_dispatch(h, dest, n_rows):
    n_tok, d = h.shape
    tt = DISPATCH_TOKENS
    steps = n_tok // tt
    zeros = jnp.zeros((n_rows, d), h.dtype)
    idx = pl.BlockSpec((1, 1, tt), lambda i: (i, 0, 0), memory_space=pltpu.SMEM)
    return pl.pallas_call(
        _dispatch_kernel,
        grid=(steps,),
        in_specs=[idx, idx,
                  pl.BlockSpec((tt, d), lambda i: (i, 0)),
                  pl.BlockSpec(memory_space=pl.ANY)],
        out_specs=pl.BlockSpec(memory_space=pl.ANY),
        out_shape=jax.ShapeDtypeStruct((n_rows, d), h.dtype),
        input_output_aliases={3: 0},
        scratch_shapes=[pltpu.SemaphoreType.DMA(())],
        compiler_params=pltpu.CompilerParams(dimension_semantics=("arbitrary",), has_side_effects=True),
        name="moe_dispatch",
    )(dest[:, 0].reshape(steps, 1, tt), dest[:, 1].reshape(steps, 1, tt), h, zeros)


def _expert_kernel(be_ref, nb_ref, xd_ref, wg_ref, wu_ref, wd_ref, o_ref, wg_b, wu_b, wd_b):
    i = pl.program_id(0)

    @pl.when(jnp.logical_or(i == 0, be_ref[i] != be_ref[jnp.maximum(i - 1, 0)]))
    def _():
        wg_b[...] = wg_ref[0, 0].astype(BF16)
        wu_b[...] = wu_ref[0, 0].astype(BF16)
        wd_b[...] = wd_ref[0, 0].astype(BF16)

    @pl.when(i < nb_ref[0])
    def _():
        xb = xd_ref[...].astype(BF16)
        gate = jnp.dot(xb, wg_b[...], preferred_element_type=F32)
        up = jnp.dot(xb, wu_b[...], preferred_element_type=F32)
        o_ref[...] = jnp.dot((_silu(gate) * up).astype(BF16), wd_b[...], preferred_element_type=F32)

    @pl.when(i >= nb_ref[0])
    def _():
        o_ref[...] = jnp.zeros(o_ref.shape, F32)


def moe_experts(xd, block_e, n_used, w_gate, w_up, w_down, layer):
    n_rows, d = xd.shape
    ff = w_gate.shape[-1]
    n_blocks = n_rows // MOE_BLOCK
    grid_spec = pltpu.PrefetchScalarGridSpec(
        num_scalar_prefetch=2,
        grid=(n_blocks,),
        in_specs=[pl.BlockSpec((MOE_BLOCK, d), lambda i, be, nb: (i, 0)),
                  pl.BlockSpec((1, 1, d, ff), lambda i, be, nb: (layer, be[i], 0, 0)),
                  pl.BlockSpec((1, 1, d, ff), lambda i, be, nb: (layer, be[i], 0, 0)),
                  pl.BlockSpec((1, 1, ff, d), lambda i, be, nb: (layer, be[i], 0, 0))],
        out_specs=pl.BlockSpec((MOE_BLOCK, d), lambda i, be, nb: (i, 0)),
        scratch_shapes=[pltpu.VMEM((d, ff), BF16), pltpu.VMEM((d, ff), BF16), pltpu.VMEM((ff, d), BF16)],
    )
    return pl.pallas_call(
        _expert_kernel,
        grid_spec=grid_spec,
        out_shape=jax.ShapeDtypeStruct((n_rows, d), F32),
        compiler_params=_params("arbitrary"),
        name="moe_experts",
    )(block_e, n_used, xd, w_gate, w_up, w_down)


COMBINE_TOKENS = 1024


def _combine_kernel(d0_ref, d1_ref, x_ref, route_ref, g2_ref, yd_ref, o_ref, rows, sem):
    def copies(j):
        return (pltpu.make_async_copy(yd_ref.at[pl.ds(d0_ref[0, 0, j], 1), :], rows.at[0, pl.ds(j, 1), :], sem),
                pltpu.make_async_copy(yd_ref.at[pl.ds(d1_ref[0, 0, j], 1), :], rows.at[1, pl.ds(j, 1), :], sem))

    def start(j, carry):
        for cp in copies(j):
            cp.start()
        return carry

    def wait(j, carry):
        for cp in copies(j):
            cp.wait()
        return carry

    lax.fori_loop(0, COMBINE_TOKENS, start, 0, unroll=8)
    lax.fori_loop(0, COMBINE_TOKENS, wait, 0, unroll=8)
    route = route_ref[...]
    w1 = route[:, ROUTE_W:ROUTE_W + 1]
    w2 = route[:, ROUTE_W + 1:ROUTE_W + 2]
    y = rows[0] * w1 + rows[1] * w2
    o_ref[...] = x_ref[...] + g2_ref[0] * y


def moe_combine(x, route, g2, yd, dest):
    bsz, seq, d = x.shape
    n_tok = bsz * seq
    tt = COMBINE_TOKENS
    per_batch = seq // tt
    steps = n_tok // tt
    idx = pl.BlockSpec((1, 1, tt), lambda i: (i, 0, 0), memory_space=pltpu.SMEM)
    return pl.pallas_call(
        _combine_kernel,
        grid=(steps,),
        in_specs=[idx, idx,
                  pl.BlockSpec((tt, d), lambda i: (i, 0)),
                  pl.BlockSpec((tt, LANES), lambda i: (i, 0)),
                  pl.BlockSpec((1, 1, d), lambda i: (i // per_batch, 0, 0)),
                  pl.BlockSpec(memory_space=pl.ANY)],
        out_specs=pl.BlockSpec((tt, d), lambda i: (i, 0)),
        out_shape=jax.ShapeDtypeStruct((n_tok, d), F32),
        scratch_shapes=[pltpu.VMEM((TOP_K, tt, d), F32), pltpu.SemaphoreType.DMA(())],
        compiler_params=_params("arbitrary"),
        name="moe_combine",
    )(dest[:, 0].reshape(steps, 1, tt), dest[:, 1].reshape(steps, 1, tt), x.reshape(n_tok, d), route,
      g2.reshape(bsz, 1, d), yd).reshape(bsz, seq, d)


def hier_moe_block(x, g, scale, shift, gate2, router_w, router_b, w_gate, w_up, w_down, layer):
    bsz, seq, d = x.shape
    n_tok = bsz * seq
    n_asg = n_tok * TOP_K
    n_blocks = -(-n_asg // MOE_BLOCK) + N_EXPERTS
    h, route, counts = moe_router(x, g, scale, shift, router_w, router_b, layer)
    counts = counts[0, MOE_GROUPS:MOE_GROUPS + N_EXPERTS].astype(I32)
    pcounts = ((counts + MOE_BLOCK - 1) // MOE_BLOCK) * MOE_BLOCK
    pend = jnp.cumsum(pcounts)
    pstart = pend - pcounts
    expert = route[:, ROUTE_E:ROUTE_E + TOP_K].astype(I32)
    rank = route[:, ROUTE_RANK:ROUTE_RANK + TOP_K].astype(I32)
    experts = jnp.arange(N_EXPERTS, dtype=I32)
    dest = rank + jnp.sum(jnp.where(expert[..., None] == experts, pstart, 0), axis=-1)
    block_rows = jnp.arange(n_blocks, dtype=I32) * MOE_BLOCK
    block_e = jnp.minimum(jnp.sum((pend[None, :] <= block_rows[:, None]).astype(I32), axis=1), N_EXPERTS - 1)
    n_used = (pend[-1:] // MOE_BLOCK).astype(I32)
    xd = moe_dispatch(h, dest, n_blocks * MOE_BLOCK)
    yd = moe_experts(xd, block_e, n_used, w_gate, w_up, w_down, layer)
    return moe_combine(x, route, gate2, yd, dest)


def kernel(x, c, positions, ada_w, ada_b, norm1_g, norm2_g, w_in, ssd_conv_w, ssd_conv_b, ssd_dt_bias, ssd_a_log, ssd_d, ssd_norm_g, lru_conv_w, lru_conv_b, lru_w_r, lru_b_r, lru_w_i, lru_b_i, lru_lambda, w_br_attn, w_br_ssd, w_br_lru, w_out, router_wg, router_bg, router_we, router_be, exp_w_gate, exp_w_up, exp_w_down, final_g):
    depth = ada_w.shape[0]
    bsz, seq, d = x.shape
    n_tok = bsz * seq
    ssd_width = ssd_norm_g.shape[-1]
    xbc_width = ssd_conv_w.shape[-1]
    ssd_heads = ssd_dt_bias.shape[-1]
    lru_width = lru_lambda.shape[-1]
    o_z = 3 * ATTN_WIDTH
    o_xbc = o_z + ssd_width
    o_dt = o_xbc + xbc_width
    o_lru = o_dt + ssd_heads
    o_merge = o_lru + 2 * lru_width
    tn_big, tn_qkv, tn_lru = 1024, ATTN_WIDTH, lru_width
    c_xbc, c_z = 0, xbc_width
    c_merge = c_z + ssd_width
    c_qkv = c_merge + 3 * d
    c_lru = -(-(c_qkv + o_z) // tn_lru) * tn_lru
    c_dt = c_lru + 2 * lru_width
    seg = lambda a, b: w_in[:, :, a:b]
    zeros = lambda n: jnp.zeros(w_in.shape[:2] + (n,), w_in.dtype)
    w_all = jnp.concatenate([seg(o_xbc, o_dt), seg(o_z, o_xbc), seg(o_merge, w_in.shape[-1]),
                             rope_slab_layout(seg(0, 2 * ATTN_WIDTH)), seg(2 * ATTN_WIDTH, o_z),
                             zeros(c_lru - c_qkv - o_z), seg(o_lru, o_merge), seg(o_dt, o_lru),
                             zeros(LANES - ssd_heads)], axis=-1).astype(BF16)
    w_a, w_s, w_l, w_o = (w.astype(BF16) for w in (w_br_attn, w_br_ssd, w_br_lru, w_out))
    lru_w = lru_gate_weights(lru_w_r, lru_w_i)
    router_w, router_b = router_params(router_wg, router_bg, router_we, router_be)

    cond = ada_cond(c, ada_w, ada_b)
    cos, sin = rope_tables(positions)
    for l in range(depth):
        sh1, sc1, g1, sh2, sc2, g2 = [cond[l, :, i * d:(i + 1) * d] for i in range(6)]
        h = norm_modulate(x, norm1_g[l], sc1, sh1, BF16)
        h2d = h.reshape(n_tok, d)
        qkv = matmul_slabs(h, w_all, l, c_qkv, o_z, tn_qkv)
        z = matmul(h2d, w_all, l, c_z, ssd_width, 2048, tn_big, BF16)
        xbc = matmul(h2d, w_all, l, c_xbc, xbc_width, 2048, tn_big, BF16)
        dt_raw = matmul(h2d, w_all, l, c_dt, LANES, 2048, LANES)
        lru_gx = matmul(h2d, w_all, l, c_lru, 2 * lru_width, 2048, tn_lru, BF16)
        merge = matmul(h2d, w_all, l, c_merge, 3 * d, 2048, tn_big, BF16)
        y_attn = dilated_attention(qkv, cos, sin)
        y_ssd = ssd_mixer(xbc.reshape(bsz, seq, -1), dt_raw.reshape(bsz, seq, -1), z.reshape(bsz, seq, -1),
                          ssd_conv_w[l], ssd_conv_b[l], ssd_dt_bias[l], ssd_a_log[l], ssd_d[l], ssd_norm_g[l])
        y_lru = rglru_mixer(lru_gx.reshape(bsz, seq, -1), lru_conv_w[l], lru_conv_b[l], lru_w, lru_b_r[l],
                            lru_b_i[l], lru_lambda[l], l)
        x = merge_block(x, y_attn, y_ssd, y_lru, merge.reshape(bsz, seq, -1), g1, w_a, w_s, w_l, w_o, l)
        x = hier_moe_block(x, norm2_g[l], sc2, sh2, g2, router_w, router_b, exp_w_gate, exp_w_up, exp_w_down, l)
    return final_norm(x, final_g)
```

```python
import functools
import math

import jax
import jax.numpy as jnp
import numpy as np
from jax import lax
from jax.experimental import pallas as pl
from jax.experimental.pallas import tpu as pltpu

F32 = jnp.float32
BF16 = jnp.bfloat16
I32 = jnp.int32

LANES = 128
SUBLANES = 8
VMEM_BYTES = 64 << 20
VMEM_LIMIT = (VMEM_BYTES * 3) // 4

EPS = 1e-6
HEAD_DIM = 64
ATTN_DILATIONS = (1, 4, 16)
ATTN_BLK = 128
ATTN_UNROLL = 4
HEADS_PER_GROUP = 8
N_ATTN_GROUPS = 3
ATTN_WIDTH = N_ATTN_GROUPS * HEADS_PER_GROUP * HEAD_DIM
ROPE_THETA = 10000.0
SSD_HEAD_DIM = 64
SSD_GROUPS = 8
SSD_HEADS_PER_GROUP = 4
SSD_STATE = 128
SSD_CHUNK = 128
CONV_K = 4
LRU_BLOCKS = 16
LRU_BLOCK_WIDTH = 80
LRU_C = 8.0
MOE_GROUPS = 4
EXPERTS_PER_GROUP = 8
N_EXPERTS = MOE_GROUPS * EXPERTS_PER_GROUP
TOP_K = 2
MOE_BLOCK = 512
NEG = -1e30


def _params(*sem):
    return pltpu.CompilerParams(dimension_semantics=sem, vmem_limit_bytes=VMEM_LIMIT)


def _sigmoid(x):
    return 0.5 * jnp.tanh(0.5 * x) + 0.5


def _silu(x):
    return x * _sigmoid(x)


def _softplus(x):
    return jnp.maximum(x, 0.0) + jnp.log1p(jnp.exp(-jnp.abs(x)))


def _ada_kernel(c_ref, w_ref, b_ref, o_ref):
    c_act = _silu(c_ref[...]).astype(BF16)
    o_ref[0] = jnp.dot(c_act, w_ref[0].astype(BF16), preferred_element_type=F32) + b_ref[0]


def ada_cond(c, ada_w, ada_b):
    depth, d_model, n = ada_w.shape
    bsz = c.shape[0]
    tn = 2048
    return pl.pallas_call(
        _ada_kernel,
        grid=(depth, n // tn),
        in_specs=[pl.BlockSpec((bsz, d_model), lambda l, j: (0, 0)),
                  pl.BlockSpec((1, d_model, tn), lambda l, j: (l, 0, j)),
                  pl.BlockSpec((1, 1, tn), lambda l, j: (l, 0, j))],
        out_specs=pl.BlockSpec((1, bsz, tn), lambda l, j: (l, 0, j)),
        out_shape=jax.ShapeDtypeStruct((depth, bsz, n), F32),
        compiler_params=_params("parallel", "parallel"),
        name="ada_cond",
    )(c, ada_w, ada_b.reshape(depth, 1, n))


def _norm_kernel(x_ref, g_ref, sc_ref, sh_ref, o_ref):
    x = x_ref[0]
    y = x * lax.rsqrt(jnp.mean(x * x, axis=-1, keepdims=True) + EPS) * g_ref[...]
    o_ref[0] = (y * (1.0 + sc_ref[0]) + sh_ref[0]).astype(o_ref.dtype)


def _plain_norm_kernel(x_ref, g_ref, o_ref):
    x = x_ref[0]
    o_ref[0] = x * lax.rsqrt(jnp.mean(x * x, axis=-1, keepdims=True) + EPS) * g_ref[...]


def norm_modulate(x, g, scale, shift, out_dtype):
    bsz, seq, d = x.shape
    ts = 512
    return pl.pallas_call(
        _norm_kernel,
        grid=(bsz, seq // ts),
        in_specs=[pl.BlockSpec((1, ts, d), lambda b, i: (b, i, 0)),
                  pl.BlockSpec((1, d), lambda b, i: (0, 0)),
                  pl.BlockSpec((1, 1, d), lambda b, i: (b, 0, 0)),
                  pl.BlockSpec((1, 1, d), lambda b, i: (b, 0, 0))],
        out_specs=pl.BlockSpec((1, ts, d), lambda b, i: (b, i, 0)),
        out_shape=jax.ShapeDtypeStruct((bsz, seq, d), out_dtype),
        compiler_params=_params("parallel", "parallel"),
        name="norm_modulate",
    )(x, g.reshape(1, d), scale.reshape(bsz, 1, d), shift.reshape(bsz, 1, d))


def final_norm(x, g):
    bsz, seq, d = x.shape
    ts = 512
    return pl.pallas_call(
        _plain_norm_kernel,
        grid=(bsz, seq // ts),
        in_specs=[pl.BlockSpec((1, ts, d), lambda b, i: (b, i, 0)),
                  pl.BlockSpec((1, d), lambda b, i: (0, 0))],
        out_specs=pl.BlockSpec((1, ts, d), lambda b, i: (b, i, 0)),
        out_shape=jax.ShapeDtypeStruct((bsz, seq, d), F32),
        compiler_params=_params("parallel", "parallel"),
        name="final_norm",
    )(x, g.reshape(1, d))


def _mm_kernel(a_ref, w_ref, o_ref):
    o_ref[...] = jnp.dot(a_ref[...], w_ref[0], preferred_element_type=F32).astype(o_ref.dtype)


def matmul(a, w_stack, layer, col0, n, tm, tn, out_dtype=F32):
    m, k = a.shape
    j0 = col0 // tn
    assert col0 % tn == 0 and n % tn == 0 and m % tm == 0
    return pl.pallas_call(
        _mm_kernel,
        grid=(m // tm, n // tn),
        in_specs=[pl.BlockSpec((tm, k), lambda i, j: (i, 0)),
                  pl.BlockSpec((1, k, tn), lambda i, j: (layer, 0, j0 + j))],
        out_specs=pl.BlockSpec((tm, tn), lambda i, j: (i, j)),
        out_shape=jax.ShapeDtypeStruct((m, n), out_dtype),
        compiler_params=_params("parallel", "parallel"),
        name="matmul",
    )(a, w_stack)


def _mm_slab_kernel(a_ref, w_ref, o_ref):
    res = jnp.dot(a_ref[0], w_ref[0], preferred_element_type=F32).astype(o_ref.dtype)
    for s in range(o_ref.shape[1]):
        o_ref[0, s] = res[:, s * LANES:(s + 1) * LANES]


def matmul_slabs(a, w_stack, layer, col0, n, tn):
    bsz, seq, k = a.shape
    j0 = col0 // tn
    assert col0 % tn == 0 and n % tn == 0
    return pl.pallas_call(
        _mm_slab_kernel,
        grid=(bsz, n // tn),
        in_specs=[pl.BlockSpec((1, seq, k), lambda b, j: (b, 0, 0)),
                  pl.BlockSpec((1, k, tn), lambda b, j: (layer, 0, j0 + j))],
        out_specs=pl.BlockSpec((1, tn // LANES, seq, LANES), lambda b, j: (b, j, 0, 0)),
        out_shape=jax.ShapeDtypeStruct((bsz, n // LANES, seq, LANES), BF16),
        compiler_params=_params("parallel", "parallel"),
        name="matmul_slabs",
    )(a, w_stack)


def _rope_table_kernel(pos_ref, freq_ref, sign_ref, cos_ref, sin_ref):
    ang = pos_ref[0] * freq_ref[...]
    cos_ref[0] = jnp.cos(ang)
    sin_ref[0] = jnp.sin(ang) * sign_ref[...]


def rope_slab_layout(w):
    half = HEAD_DIM // 2
    lead = w.shape[:-1]
    w = w.reshape(lead + (w.shape[-1] // LANES, 2, 2, half))
    return jnp.swapaxes(w, -2, -3).reshape(lead + (-1,))


def rope_tables(positions):
    bsz, seq = positions.shape
    half = HEAD_DIM // 2
    freqs = ROPE_THETA ** (-jnp.arange(half, dtype=F32) / half)
    freq_row = jnp.tile(freqs, LANES // half).reshape(1, LANES)
    sign_row = jnp.concatenate([-jnp.ones((LANES // 2,), F32), jnp.ones((LANES // 2,), F32)])
    pos = jnp.broadcast_to(positions.astype(F32)[..., None], (bsz, seq, LANES))
    ts = 512
    spec = pl.BlockSpec((1, ts, LANES), lambda b, i: (b, i, 0))
    row = pl.BlockSpec((1, LANES), lambda b, i: (0, 0))
    return pl.pallas_call(
        _rope_table_kernel,
        grid=(bsz, seq // ts),
        in_specs=[spec, row, row],
        out_specs=[spec, spec],
        out_shape=[jax.ShapeDtypeStruct((bsz, seq, LANES), F32)] * 2,
        compiler_params=_params("parallel", "parallel"),
        name="rope_tables",
    )(pos, freq_row, sign_row.reshape(1, LANES))


def _attn_kernel(q1, k1, v1, q2, k2, v2, q3, k3, v3, cos_ref, sin_ref, o_ref, qr, kr, vr, og, lg):
    seq = o_ref.shape[1]
    blk = ATTN_BLK
    pad = kr.shape[0] - seq
    rope_rows = 256
    lane_b = lax.broadcasted_iota(I32, (blk, LANES), 1)
    head0 = lane_b < HEAD_DIM
    qk_head0 = (lane_b % HEAD_DIM) < (HEAD_DIM // 2)
    qi = lax.broadcasted_iota(I32, (2 * blk, 2 * blk), 0) % blk
    ki = lax.broadcasted_iota(I32, (2 * blk, 2 * blk), 1)
    bias_prev = jnp.where(ki >= qi, 0.0, NEG).astype(F32)
    bias_cur = jnp.where(ki - blk <= qi, 0.0, NEG).astype(F32)
    bias_inner = jnp.where(ki < blk, bias_prev, bias_cur)
    bias_first = jnp.where(ki < blk, NEG, bias_cur)
    nt = (((1,), (1,)), ((), ()))
    ones = jnp.ones((2 * blk, LANES), BF16)
    kr[0:pad, :] = jnp.zeros((pad, LANES), F32)
    vr[0:pad, :] = jnp.zeros((pad, LANES), F32)

    for g, (q_ref, k_ref, v_ref) in enumerate(((q1, k1, v1), (q2, k2, v2), (q3, k3, v3))):
        dil = ATTN_DILATIONS[g]
        n_blk = seq // (dil * blk)

        def rope_body(i, carry, q_ref=q_ref, k_ref=k_ref, v_ref=v_ref):
            rows = pl.ds(pl.multiple_of(i * rope_rows, rope_rows), rope_rows)
            rows_p = pl.ds(pl.multiple_of(pad + i * rope_rows, rope_rows), rope_rows)
            cos = cos_ref[0, rows, :]
            sin = sin_ref[0, rows, :]
            for src, scale in ((q_ref, HEAD_DIM ** -0.5), (k_ref, None)):
                t = src[0, 0, rows, :].astype(F32)
                r = t * cos + pltpu.roll(t, LANES // 2, 1) * sin
                if scale is None:
                    kr[rows_p, :] = r
                else:
                    qr[rows, :] = r * scale
            vr[rows_p, :] = v_ref[0, 0, rows, :].astype(F32)
            return carry

        lax.fori_loop(0, seq // rope_rows, rope_body, 0)

        def blk_body(it, carry, dil=dil, n_blk=n_blk, g=g):
            blocks = []
            for u in range(ATTN_UNROLL):
                idx = it * ATTN_UNROLL + u
                phase = idx // n_blk
                n = idx % n_blk
                start = phase + dil * blk * n
                cur = pl.ds(start, blk, stride=dil) if dil > 1 else pl.ds(start, blk)
                both = (pl.ds(pad + start - dil * blk, 2 * blk, stride=dil) if dil > 1
                        else pl.ds(pad + start - blk, 2 * blk))
                blocks.append((cur, both, n))
            scores = []
            for cur, both, n in blocks:
                q = qr[cur, :]
                q2 = jnp.concatenate([jnp.where(qk_head0, q, 0.0), jnp.where(qk_head0, 0.0, q)], axis=0).astype(BF16)
                s = lax.dot_general(q2, kr[both, :].astype(BF16), nt, preferred_element_type=F32)
                scores.append(s + jnp.where(n > 0, bias_inner, bias_first))
            probs = []
            for s in scores:
                m = jnp.max(jnp.maximum(s[:, :blk], s[:, blk:]), axis=-1, keepdims=True)
                probs.append((m, jnp.exp(s - m).astype(BF16)))
            for (cur, both, _), (m, p) in zip(blocks, probs):
                v1 = jnp.concatenate([vr[both, :].astype(BF16), ones], axis=1)
                r = jnp.dot(p, v1, preferred_element_type=F32)
                den = jnp.where(head0, r[:blk, LANES:], r[blk:, LANES:])
                og[g, cur, :] = jnp.where(head0, r[:blk, :LANES], r[blk:, :LANES]) / den
                m_b = jnp.broadcast_to(m, (2 * blk, LANES))
                lg[g, cur, :] = jnp.where(head0, m_b[:blk], m_b[blk:]) + jnp.log(den)
            return carry

        lax.fori_loop(0, dil * n_blk // ATTN_UNROLL, blk_body, 0)

    def mix_body(i, carry):
        rows = pl.ds(pl.multiple_of(i * rope_rows, rope_rows), rope_rows)
        l0, l1, l2 = lg[0, rows, :], lg[1, rows, :], lg[2, rows, :]
        m = jnp.maximum(jnp.maximum(l0, l1), l2)
        e0, e1, e2 = jnp.exp(l0 - m), jnp.exp(l1 - m), jnp.exp(l2 - m)
        tot = e0 + e1 + e2
        y = (e0 / tot) * og[0, rows, :] + (e1 / tot) * og[1, rows, :] + (e2 / tot) * og[2, rows, :]
        o_ref[0, rows, :] = y.astype(o_ref.dtype)
        return carry

    lax.fori_loop(0, seq // rope_rows, mix_body, 0)


def dilated_attention(qkv, cos, sin):
    bsz, n_slabs, seq, _ = qkv.shape
    pairs = HEADS_PER_GROUP * HEAD_DIM // LANES
    per_proj = n_slabs // 3
    pad = ATTN_BLK * max(ATTN_DILATIONS)

    def slab(proj, g):
        return pl.BlockSpec((1, 1, seq, LANES), lambda b, j: (b, proj * per_proj + g * pairs + j, 0, 0))

    in_specs = []
    for g in range(N_ATTN_GROUPS):
        in_specs += [slab(0, g), slab(1, g), slab(2, g)]
    table = pl.BlockSpec((1, seq, LANES), lambda b, j: (b, 0, 0))
    in_specs += [table, table]
    return pl.pallas_call(
        _attn_kernel,
        grid=(bsz, pairs),
        in_specs=in_specs,
        out_specs=pl.BlockSpec((1, seq, LANES), lambda b, j: (b, 0, j)),
        out_shape=jax.ShapeDtypeStruct((bsz, seq, pairs * LANES), BF16),
        scratch_shapes=[pltpu.VMEM((seq, LANES), F32), pltpu.VMEM((pad + seq, LANES), F32),
                        pltpu.VMEM((pad + seq, LANES), F32),
                        pltpu.VMEM((N_ATTN_GROUPS, seq, LANES), F32), pltpu.VMEM((N_ATTN_GROUPS, seq, LANES), F32)],
        compiler_params=_params("parallel", "parallel"),
        name="dilated_attention",
    )(*([qkv] * 9), cos, sin)


def _causal_conv(buf, x, w_ref, b_ref, rows):
    buf[SUBLANES:SUBLANES + rows, :] = x
    acc = b_ref[...] + w_ref[CONV_K - 1:CONV_K, :] * x
    for k in range(CONV_K - 1):
        off = SUBLANES - (CONV_K - 1) + k
        acc = acc + w_ref[k:k + 1, :] * buf[off:off + rows, :]
    buf[0:SUBLANES, :] = buf[rows:rows + SUBLANES, :]
    return acc


def _by_head(cols, lane, width):
    out = cols[-1]
    for j in range(len(cols) - 2, -1, -1):
        out = jnp.where(lane < (j + 1) * width, cols[j], out)
    return out


def _ssd_kernel(xbc_ref, dt_ref, z_ref, cw_ref, cb_ref, dtb_ref, a_ref, dskip_ref, ng_ref, o_ref,
                buf, state, ybuf):
    chunk = SSD_CHUNK
    width = SSD_GROUPS * SSD_HEADS_PER_GROUP * SSD_HEAD_DIM
    gw = SSD_HEADS_PER_GROUP * SSD_HEAD_DIM

    @pl.when(pl.program_id(1) == 0)
    def _():
        buf[0:SUBLANES, :] = jnp.zeros((SUBLANES, buf.shape[1]), F32)
        state[...] = jnp.zeros(state.shape, F32)

    xbc = _silu(_causal_conv(buf, xbc_ref[0].astype(F32), cw_ref, cb_ref, chunk))
    dt = _softplus(dt_ref[0] + dtb_ref[...])
    a = dt * a_ref[...]
    li = lax.broadcasted_iota(I32, (chunk, chunk), 0)
    si = lax.broadcasted_iota(I32, (chunk, chunk), 1)
    tril = li >= si
    tri = jnp.where(tril, 1.0, 0.0).astype(BF16)
    a_hi = a.astype(BF16)
    r1 = a - a_hi.astype(F32)
    a_mid = r1.astype(BF16)
    a_lo = (r1 - a_mid.astype(F32)).astype(BF16)
    a_cs = (jnp.dot(tri, a_hi, preferred_element_type=F32) + jnp.dot(tri, a_mid, preferred_element_type=F32)
            + jnp.dot(tri, a_lo, preferred_element_type=F32))
    a_cs_t = a_cs.T
    a_last = a_cs[chunk - 1:chunk, :]
    e_acs = jnp.exp(a_cs)
    e_dec = jnp.exp(a_last - a_cs)
    e_last = jnp.exp(a_last)
    lane_g = lax.broadcasted_iota(I32, (chunk, gw), 1)
    lane_1 = lax.broadcasted_iota(I32, (1, gw), 1)

    for g in range(SSD_GROUPS):
        heads = range(g * SSD_HEADS_PER_GROUP, (g + 1) * SSD_HEADS_PER_GROUP)
        xs = xbc[:, g * gw:(g + 1) * gw]
        bm = xbc[:, width + g * SSD_STATE: width + (g + 1) * SSD_STATE]
        cm = xbc[:, width + SSD_GROUPS * SSD_STATE + g * SSD_STATE: width + SSD_GROUPS * SSD_STATE + (g + 1) * SSD_STATE]
        bm_t = bm.T.astype(BF16)
        cm_b = cm.astype(BF16)
        cb = jnp.dot(cm_b, bm_t, preferred_element_type=F32)
        dt_g = _by_head([dt[:, h:h + 1] for h in heads], lane_g, SSD_HEAD_DIM)
        x_dt = xs * dt_g
        x_dt_b = x_dt.astype(BF16)
        diag = []
        for h in heads:
            l_mat = jnp.where(tril, jnp.exp(a_cs[:, h:h + 1] - a_cs_t[h:h + 1, :]), 0.0)
            diag.append(jnp.dot((cb * l_mat).astype(BF16), x_dt_b, preferred_element_type=F32))
        y = _by_head(diag, lane_g, SSD_HEAD_DIM)
        s_prev = state[g]
        y = y + jnp.dot(cm_b, s_prev.astype(BF16), preferred_element_type=F32) * _by_head(
            [e_acs[:, h:h + 1] for h in heads], lane_g, SSD_HEAD_DIM)
        x_dec = (x_dt * _by_head([e_dec[:, h:h + 1] for h in heads], lane_g, SSD_HEAD_DIM)).astype(BF16)
        state[g] = (s_prev * _by_head([e_last[:, h:h + 1] for h in heads], lane_1, SSD_HEAD_DIM)
                    + jnp.dot(bm_t, x_dec, preferred_element_type=F32))
        ybuf[:, g * gw:(g + 1) * gw] = y + dskip_ref[:, g * gw:(g + 1) * gw] * xs

    v = ybuf[...] * _silu(z_ref[0].astype(F32))
    o_ref[0] = (v * lax.rsqrt(jnp.mean(v * v, axis=-1, keepdims=True) + EPS) * ng_ref[...]).astype(o_ref.dtype)


def ssd_mixer(xbc, dt_raw, z, conv_w, conv_b, dt_bias, a_log, d_skip, norm_g):
    bsz, seq, cw = xbc.shape
    width = z.shape[-1]
    heads = dt_bias.shape[0]
    pad = LANES - heads
    dtb = jnp.pad(dt_bias, (0, pad)).reshape(1, LANES)
    a_neg = jnp.pad(-jnp.exp(a_log), (0, pad)).reshape(1, LANES)
    dskip = jnp.repeat(d_skip, SSD_HEAD_DIM).reshape(1, width)
    row = lambda n: pl.BlockSpec((1, n), lambda b, c: (0, 0))
    return pl.pallas_call(
        _ssd_kernel,
        grid=(bsz, seq // SSD_CHUNK),
        in_specs=[pl.BlockSpec((1, SSD_CHUNK, cw), lambda b, c: (b, c, 0)),
                  pl.BlockSpec((1, SSD_CHUNK, LANES), lambda b, c: (b, c, 0)),
                  pl.BlockSpec((1, SSD_CHUNK, width), lambda b, c: (b, c, 0)),
                  pl.BlockSpec((CONV_K, cw), lambda b, c: (0, 0)),
                  row(cw), row(LANES), row(LANES), row(width), row(width)],
        out_specs=pl.BlockSpec((1, SSD_CHUNK, width), lambda b, c: (b, c, 0)),
        out_shape=jax.ShapeDtypeStruct((bsz, seq, width), BF16),
        scratch_shapes=[pltpu.VMEM((SSD_CHUNK + SUBLANES, cw), F32),
                        pltpu.VMEM((SSD_GROUPS, SSD_STATE, SSD_HEADS_PER_GROUP * SSD_HEAD_DIM), F32),
                        pltpu.VMEM((SSD_CHUNK, width), F32)],
        compiler_params=_params("parallel", "arbitrary"),
        name="ssd_mixer",
    )(xbc, dt_raw, z, conv_w, conv_b.reshape(1, cw), dtb, a_neg, dskip, norm_g.reshape(1, width))


def _lru_kernel(x_ref, gate_ref, cw_ref, cb_ref, w_ref, br_ref, bi_ref, lam_ref, o_ref, buf, a_s, u_s, h_s):
    rows = x_ref.shape[1]
    width = x_ref.shape[2]
    half = width // 2

    @pl.when(pl.program_id(1) == 0)
    def _():
        buf[0:SUBLANES, :] = jnp.zeros((SUBLANES, width), F32)
        h_s[...] = jnp.zeros(h_s.shape, F32)

    xc = _causal_conv(buf, x_ref[0].astype(F32), cw_ref, cb_ref, rows)
    sp = _softplus(-lam_ref[...])
    for c in range(2):
        cols = slice(c * half, (c + 1) * half)
        pre = jnp.dot(xc[:, cols].astype(BF16), w_ref[0, c], preferred_element_type=F32)
        r_gate = _sigmoid(pre[:, :half] + br_ref[:, cols])
        i_gate = _sigmoid(pre[:, half:] + bi_ref[:, cols])
        log_a = -LRU_C * r_gate * sp[:, cols]
        t = jnp.tanh(log_a)
        one_minus_a2 = -2.0 * t / (1.0 - t)
        a_s[:, cols] = jnp.exp(log_a)
        u_s[:, cols] = jnp.sqrt(one_minus_a2) * (i_gate * xc[:, cols])

    sub = lax.broadcasted_iota(I32, (SUBLANES, width), 0)

    def scan_body(i, h):
        tile = pl.ds(pl.multiple_of(i * SUBLANES, SUBLANES), SUBLANES)
        a = a_s[tile, :]
        u = u_s[tile, :]
        for s in (1, 2, 4):
            keep = sub >= s
            a_sh = jnp.where(keep, pltpu.roll(a, s, 0), 1.0)
            u_sh = jnp.where(keep, pltpu.roll(u, s, 0), 0.0)
            u = a * u_sh + u
            a = a * a_sh
        hs = u + a * h
        u_s[tile, :] = hs
        return jnp.broadcast_to(hs[SUBLANES - 1:SUBLANES, :], (SUBLANES, width))

    h_s[...] = lax.fori_loop(0, rows // SUBLANES, scan_body, h_s[...])
    o_ref[0] = (u_s[...] * jax.nn.gelu(gate_ref[0].astype(F32), approximate=True)).astype(o_ref.dtype)


def lru_gate_weights(w_r, w_i):
    depth = w_r.shape[0]
    per_half = LRU_BLOCKS // 2
    half = per_half * LRU_BLOCK_WIDTH

    def block_diag(w):
        eye = jnp.eye(per_half, dtype=w.dtype)
        wb = w.reshape(depth, 2, per_half, LRU_BLOCK_WIDTH, LRU_BLOCK_WIDTH)
        return jnp.einsum('dckij,kl->dckilj', wb, eye).reshape(depth, 2, half, half)

    return jnp.concatenate([block_diag(w_r), block_diag(w_i)], axis=-1).astype(BF16)


def rglru_mixer(lru_gx, conv_w, conv_b, w_cat, b_r, b_i, lam, layer):
    bsz, seq, w2 = lru_gx.shape
    width = w2 // 2
    half = width // 2
    ts = 256
    row = pl.BlockSpec((1, width), lambda b, i: (0, 0))
    return pl.pallas_call(
        _lru_kernel,
        grid=(bsz, seq // ts),
        in_specs=[pl.BlockSpec((1, ts, width), lambda b, i: (b, i, 1)),
                  pl.BlockSpec((1, ts, width), lambda b, i: (b, i, 0)),
                  pl.BlockSpec((CONV_K, width), lambda b, i: (0, 0)),
                  row,
                  pl.BlockSpec((1, 2, half, width), lambda b, i: (layer, 0, 0, 0)),
                  row, row, row],
        out_specs=pl.BlockSpec((1, ts, width), lambda b, i: (b, i, 0)),
        out_shape=jax.ShapeDtypeStruct((bsz, seq, width), BF16),
        scratch_shapes=[pltpu.VMEM((ts + SUBLANES, width), F32), pltpu.VMEM((ts, width), F32),
                        pltpu.VMEM((ts, width), F32), pltpu.VMEM((SUBLANES, width), F32)],
        compiler_params=_params("parallel", "arbitrary"),
        name="rglru_mixer",
    )(lru_gx, lru_gx, conv_w, conv_b.reshape(1, width), w_cat, b_r.reshape(1, width), b_i.reshape(1, width),
      lam.reshape(1, width))


def _merge_kernel(x_ref, ya_ref, ys_ref, yl_ref, mg_ref, g1_ref, wa_ref, ws_ref, wl_ref, wo_ref, o_ref):
    d = x_ref.shape[2]
    gates = _sigmoid(mg_ref[0].astype(F32))
    merged = (gates[:, :d] * jnp.dot(ya_ref[0], wa_ref[0], preferred_element_type=F32)
              + gates[:, d:2 * d] * jnp.dot(ys_ref[0], ws_ref[0], preferred_element_type=F32)
              + gates[:, 2 * d:] * jnp.dot(yl_ref[0], wl_ref[0], preferred_element_type=F32))
    y = jnp.dot(merged.astype(BF16), wo_ref[0], preferred_element_type=F32)
    o_ref[0] = x_ref[0] + g1_ref[0] * y


def merge_block(x, y_attn, y_ssd, y_lru, merge, g1, w_a, w_s, w_l, w_o, layer):
    bsz, seq, d = x.shape
    ts = 512
    tok = lambda n: pl.BlockSpec((1, ts, n), lambda b, i: (b, i, 0))
    full = lambda w: pl.BlockSpec((1,) + w.shape[1:], lambda b, i: (layer, 0, 0))
    return pl.pallas_call(
        _merge_kernel,
        grid=(bsz, seq // ts),
        in_specs=[tok(d), tok(y_attn.shape[-1]), tok(y_ssd.shape[-1]), tok(y_lru.shape[-1]), tok(merge.shape[-1]),
                  pl.BlockSpec((1, 1, d), lambda b, i: (b, 0, 0)),
                  full(w_a), full(w_s), full(w_l), full(w_o)],
        out_specs=tok(d),
        out_shape=jax.ShapeDtypeStruct((bsz, seq, d), F32),
        compiler_params=_params("parallel", "parallel"),
        name="merge_block",
    )(x, y_attn, y_ssd, y_lru, merge, g1.reshape(bsz, 1, d), w_a, w_s, w_l, w_o)


ROUTE_E, ROUTE_W, ROUTE_RANK = 0, 2, 4


def _router_kernel(x_ref, g_ref, sc_ref, sh_ref, wr_ref, br_ref, h_ref, route_ref, count_ref, run):
    ts = x_ref.shape[0]

    @pl.when(pl.program_id(0) == 0)
    def _():
        run[...] = jnp.zeros(run.shape, F32)

    x = x_ref[...]
    y = x * lax.rsqrt(jnp.mean(x * x, axis=-1, keepdims=True) + EPS) * g_ref[...]
    h = y * (1.0 + sc_ref[0]) + sh_ref[0]
    h_ref[...] = h
    h_hi = h.astype(BF16)
    r1 = h - h_hi.astype(F32)
    h_mid = r1.astype(BF16)
    h_lo = (r1 - h_mid.astype(F32)).astype(BF16)
    w_hi = wr_ref[0, 0]
    w_lo = wr_ref[0, 1]
    logits = (jnp.dot(h_hi, w_hi, preferred_element_type=F32) + jnp.dot(h_hi, w_lo, preferred_element_type=F32)
              + jnp.dot(h_mid, w_hi, preferred_element_type=F32) + jnp.dot(h_mid, w_lo, preferred_element_type=F32)
              + jnp.dot(h_lo, w_hi, preferred_element_type=F32)) + br_ref[0]
    lane = lax.broadcasted_iota(I32, (ts, LANES), 1)
    big = jnp.int32(LANES)
    lg = jnp.where(lane < MOE_GROUPS, logits, NEG)
    mg = jnp.max(lg, axis=-1, keepdims=True)
    g_sel = jnp.min(jnp.where(lg == mg, lane, big), axis=-1, keepdims=True)
    p_grp = 1.0 / jnp.sum(jnp.exp(lg - mg), axis=-1, keepdims=True)
    lo = MOE_GROUPS + EXPERTS_PER_GROUP * g_sel
    in_grp = (lane >= lo) & (lane < lo + EXPERTS_PER_GROUP)
    le = jnp.where(in_grp, logits, NEG)
    m1 = jnp.max(le, axis=-1, keepdims=True)
    i1 = jnp.min(jnp.where(le == m1, lane, big), axis=-1, keepdims=True)
    den = jnp.sum(jnp.exp(le - m1), axis=-1, keepdims=True)
    le2 = jnp.where(lane == i1, NEG, le)
    m2 = jnp.max(le2, axis=-1, keepdims=True)
    i2 = jnp.min(jnp.where(le2 == m2, lane, big), axis=-1, keepdims=True)
    p1 = 1.0 / den
    p2 = jnp.exp(m2 - m1) / den
    w1 = p_grp * (p1 / (p1 + p2))
    w2 = p_grp * (p2 / (p1 + p2))
    hot1 = lane == i1
    hot2 = lane == i2
    onehot = jnp.where(hot1 | hot2, 1.0, 0.0)
    ri = lax.broadcasted_iota(I32, (ts, ts), 0)
    ci = lax.broadcasted_iota(I32, (ts, ts), 1)
    strict = jnp.where(ri > ci, 1.0, 0.0).astype(BF16)
    before = jnp.dot(strict, onehot.astype(BF16), preferred_element_type=F32) + run[...]
    rank1 = jnp.sum(jnp.where(hot1, before, 0.0), axis=-1, keepdims=True)
    rank2 = jnp.sum(jnp.where(hot2, before, 0.0), axis=-1, keepdims=True)
    run[...] = run[...] + jnp.sum(onehot, axis=0, keepdims=True)
    count_ref[...] = run[...]
    e1 = (i1 - MOE_GROUPS).astype(F32)
    e2 = (i2 - MOE_GROUPS).astype(F32)
    packed = jnp.where(lane == ROUTE_E, e1, 0.0)
    packed = jnp.where(lane == ROUTE_E + 1, e2, packed)
    packed = jnp.where(lane == ROUTE_W, w1, packed)
    packed = jnp.where(lane == ROUTE_W + 1, w2, packed)
    packed = jnp.where(lane == ROUTE_RANK, rank1, packed)
    packed = jnp.where(lane == ROUTE_RANK + 1, rank2, packed)
    route_ref[...] = packed


def router_params(router_wg, router_bg, router_we, router_be):
    depth, d, _ = router_wg.shape
    w_all = jnp.concatenate([router_wg, router_we.reshape(depth, d, N_EXPERTS)], axis=-1)
    w_all = jnp.pad(w_all, ((0, 0), (0, 0), (0, LANES - w_all.shape[-1])))
    w_hi = w_all.astype(BF16)
    w_lo = (w_all - w_hi.astype(F32)).astype(BF16)
    b_all = jnp.concatenate([router_bg, router_be.reshape(depth, N_EXPERTS)], axis=-1)
    b_all = jnp.pad(b_all, ((0, 0), (0, LANES - b_all.shape[-1]))).reshape(depth, 1, LANES)
    return jnp.stack([w_hi, w_lo], axis=1), b_all


def moe_router(x, g, scale, shift, w_stack, b_stack, layer):
    bsz, seq, d = x.shape
    n_tok = bsz * seq
    ts = 512
    per_batch = seq // ts
    return pl.pallas_call(
        _router_kernel,
        grid=(n_tok // ts,),
        in_specs=[pl.BlockSpec((ts, d), lambda i: (i, 0)),
                  pl.BlockSpec((1, d), lambda i: (0, 0)),
                  pl.BlockSpec((1, 1, d), lambda i: (i // per_batch, 0, 0)),
                  pl.BlockSpec((1, 1, d), lambda i: (i // per_batch, 0, 0)),
                  pl.BlockSpec((1, 2, d, LANES), lambda i: (layer, 0, 0, 0)),
                  pl.BlockSpec((1, 1, LANES), lambda i: (layer, 0, 0))],
        out_specs=[pl.BlockSpec((ts, d), lambda i: (i, 0)),
                   pl.BlockSpec((ts, LANES), lambda i: (i, 0)),
                   pl.BlockSpec((1, LANES), lambda i: (0, 0))],
        out_shape=[jax.ShapeDtypeStruct((n_tok, d), F32),
                   jax.ShapeDtypeStruct((n_tok, LANES), F32),
                   jax.ShapeDtypeStruct((1, LANES), F32)],
        scratch_shapes=[pltpu.VMEM((1, LANES), F32)],
        compiler_params=_params("arbitrary"),
        name="moe_router",
    )(x.reshape(n_tok, d), g.reshape(1, d), scale.reshape(bsz, 1, d), shift.reshape(bsz, 1, d), w_stack, b_stack)


DISPATCH_TOKENS = 1024


def _dispatch_kernel(fill_ref, d0_ref, d1_ref, h_ref, xd_ref, zbuf, sem, zsem):
    @pl.when(pl.program_id(0) == 0)
    def _():
        zbuf[...] = jnp.zeros(zbuf.shape, zbuf.dtype)

        def zero_tile(e):
            row = pl.multiple_of(jnp.maximum(fill_ref[e], 0), MOE_BLOCK)
            return pltpu.make_async_copy(zbuf, xd_ref.at[pl.ds(row, MOE_BLOCK), :], zsem)

        for e in range(fill_ref.shape[0]):
            @pl.when(fill_ref[e] >= 0)
            def _(e=e):
                zero_tile(e).start()
        for e in range(fill_ref.shape[0]):
            @pl.when(fill_ref[e] >= 0)
            def _(e=e):
                zero_tile(e).wait()

    def copies(j):
        src = h_ref.at[pl.ds(j, 1), :]
        return (pltpu.make_async_copy(src, xd_ref.at[pl.ds(d0_ref[0, 0, j], 1), :], sem),
                pltpu.make_async_copy(src, xd_ref.at[pl.ds(d1_ref[0, 0, j], 1), :], sem))

    def start(j, carry):
        for cp in copies(j):
            cp.start()
        return carry

    def wait(j, carry):
        for cp in copies(j):
            cp.wait()
        return carry

    lax.fori_loop(0, DISPATCH_TOKENS, start, 0, unroll=8)
    lax.fori_loop(0, DISPATCH_TOKENS, wait, 0, unroll=8)


def moe_dispatch(h, dest, fill, n_rows):
    n_tok, d = h.shape
    tt = DISPATCH_TOKENS
    steps = n_tok // tt
    idx = pl.BlockSpec((1, 1, tt), lambda i: (i, 0, 0), memory_space=pltpu.SMEM)
    return pl.pallas_call(
        _dispatch_kernel,
        grid=(steps,),
        in_specs=[pl.BlockSpec(memory_space=pltpu.SMEM), idx, idx,
                  pl.BlockSpec((tt, d), lambda i: (i, 0))],
        out_specs=pl.BlockSpec(memory_space=pl.ANY),
        out_shape=jax.ShapeDtypeStruct((n_rows, d), h.dtype),
        scratch_shapes=[pltpu.VMEM((MOE_BLOCK, d), h.dtype), pltpu.SemaphoreType.DMA(()),
                        pltpu.SemaphoreType.DMA(())],
        compiler_params=pltpu.CompilerParams(dimension_semantics=("arbitrary",), has_side_effects=True,
                                             vmem_limit_bytes=VMEM_LIMIT),
        name="moe_dispatch",
    )(fill, dest[:, 0].reshape(steps, 1, tt), dest[:, 1].reshape(steps, 1, tt), h)


def _expert_kernel(be_ref, nb_ref, xd_ref, wg_ref, wu_ref, wd_ref, o_ref, wg_b, wu_b, wd_b):
    i = pl.program_id(0)

    @pl.when(jnp.logical_or(i == 0, be_ref[i] != be_ref[jnp.maximum(i - 1, 0)]))
    def _():
        wg_b[...] = wg_ref[0, 0].astype(BF16)
        wu_b[...] = wu_ref[0, 0].astype(BF16)
        wd_b[...] = wd_ref[0, 0].astype(BF16)

    @pl.when(i < nb_ref[0])
    def _():
        xb = xd_ref[...].astype(BF16)
        gate = jnp.dot(xb, wg_b[...], preferred_element_type=F32)
        up = jnp.dot(xb, wu_b[...], preferred_element_type=F32)
        o_ref[...] = jnp.dot((_silu(gate) * up).astype(BF16), wd_b[...], preferred_element_type=F32)

    @pl.when(i >= nb_ref[0])
    def _():
        o_ref[...] = jnp.zeros(o_ref.shape, F32)


def moe_experts(xd, block_e, n_used, w_gate, w_up, w_down, layer):
    n_rows, d = xd.shape
    ff = w_gate.shape[-1]
    n_blocks = n_rows // MOE_BLOCK
    grid_spec = pltpu.PrefetchScalarGridSpec(
        num_scalar_prefetch=2,
        grid=(n_blocks,),
        in_specs=[pl.BlockSpec((MOE_BLOCK, d), lambda i, be, nb: (i, 0)),
                  pl.BlockSpec((1, 1, d, ff), lambda i, be, nb: (layer, be[i], 0, 0)),
                  pl.BlockSpec((1, 1, d, ff), lambda i, be, nb: (layer, be[i], 0, 0)),
                  pl.BlockSpec((1, 1, ff, d), lambda i, be, nb: (layer, be[i], 0, 0))],
        out_specs=pl.BlockSpec((MOE_BLOCK, d), lambda i, be, nb: (i, 0)),
        scratch_shapes=[pltpu.VMEM((d, ff), BF16), pltpu.VMEM((d, ff), BF16), pltpu.VMEM((ff, d), BF16)],
    )
    return pl.pallas_call(
        _expert_kernel,
        grid_spec=grid_spec,
        out_shape=jax.ShapeDtypeStruct((n_rows, d), F32),
        compiler_params=_params("arbitrary"),
        name="moe_experts",
    )(block_e, n_used, xd, w_gate, w_up, w_down)


COMBINE_TOKENS = 1024


def _combine_kernel(d0_ref, d1_ref, x_ref, route_ref, g2_ref, yd_ref, o_ref, rows, sem):
    def copies(j):
        return (pltpu.make_async_copy(yd_ref.at[pl.ds(d0_ref[0, 0, j], 1), :], rows.at[0, pl.ds(j, 1), :], sem),
                pltpu.make_async_copy(yd_ref.at[pl.ds(d1_ref[0, 0, j], 1), :], rows.at[1, pl.ds(j, 1), :], sem))

    def start(j, carry):
        for cp in copies(j):
            cp.start()
        return carry

    def wait(j, carry):
        for cp in copies(j):
            cp.wait()
        return carry

    lax.fori_loop(0, COMBINE_TOKENS, start, 0, unroll=8)
    lax.fori_loop(0, COMBINE_TOKENS, wait, 0, unroll=8)
    route = route_ref[...]
    w1 = route[:, ROUTE_W:ROUTE_W + 1]
    w2 = route[:, ROUTE_W + 1:ROUTE_W + 2]
    y = rows[0] * w1 + rows[1] * w2
    o_ref[...] = x_ref[...] + g2_ref[0] * y


def moe_combine(x, route, g2, yd, dest):
    bsz, seq, d = x.shape
    n_tok = bsz * seq
    tt = COMBINE_TOKENS
    per_batch = seq // tt
    steps = n_tok // tt
    idx = pl.BlockSpec((1, 1, tt), lambda i: (i, 0, 0), memory_space=pltpu.SMEM)
    return pl.pallas_call(
        _combine_kernel,
        grid=(steps,),
        in_specs=[idx, idx,
                  pl.BlockSpec((tt, d), lambda i: (i, 0)),
                  pl.BlockSpec((tt, LANES), lambda i: (i, 0)),
                  pl.BlockSpec((1, 1, d), lambda i: (i // per_batch, 0, 0)),
                  pl.BlockSpec(memory_space=pl.ANY)],
        out_specs=pl.BlockSpec((tt, d), lambda i: (i, 0)),
        out_shape=jax.ShapeDtypeStruct((n_tok, d), F32),
        scratch_shapes=[pltpu.VMEM((TOP_K, tt, d), F32), pltpu.SemaphoreType.DMA(())],
        compiler_params=_params("arbitrary"),
        name="moe_combine",
    )(dest[:, 0].reshape(steps, 1, tt), dest[:, 1].reshape(steps, 1, tt), x.reshape(n_tok, d), route,
      g2.reshape(bsz, 1, d), yd).reshape(bsz, seq, d)


def hier_moe_block(x, g, scale, shift, gate2, router_w, router_b, w_gate, w_up, w_down, layer):
    bsz, seq, d = x.shape
    n_tok = bsz * seq
    n_asg = n_tok * TOP_K
    n_blocks = -(-n_asg // MOE_BLOCK) + N_EXPERTS
    h, route, counts = moe_router(x, g, scale, shift, router_w, router_b, layer)
    counts = counts[0, MOE_GROUPS:MOE_GROUPS + N_EXPERTS].astype(I32)
    pcounts = ((counts + MOE_BLOCK - 1) // MOE_BLOCK) * MOE_BLOCK
    pend = jnp.cumsum(pcounts)
    pstart = pend - pcounts
    expert = route[:, ROUTE_E:ROUTE_E + TOP_K].astype(I32)
    rank = route[:, ROUTE_RANK:ROUTE_RANK + TOP_K].astype(I32)
    experts = jnp.arange(N_EXPERTS, dtype=I32)
    dest = rank + jnp.sum(jnp.where(expert[..., None] == experts, pstart, 0), axis=-1)
    block_rows = jnp.arange(n_blocks, dtype=I32) * MOE_BLOCK
    block_e = jnp.minimum(jnp.sum((pend[None, :] <= block_rows[:, None]).astype(I32), axis=1), N_EXPERTS - 1)
    n_used = (pend[-1:] // MOE_BLOCK).astype(I32)
    trail = pend[-1] + jnp.arange(N_EXPERTS, dtype=I32) * MOE_BLOCK
    fill = jnp.concatenate([jnp.where(pcounts > 0, pend - MOE_BLOCK, -1),
                            jnp.where(trail < n_blocks * MOE_BLOCK, trail, -1)]).astype(I32)
    xd = moe_dispatch(h, dest, fill, n_blocks * MOE_BLOCK)
    yd = moe_experts(xd, block_e, n_used, w_gate, w_up, w_down, layer)
    return moe_combine(x, route, gate2, yd, dest)


def kernel(x, c, positions, ada_w, ada_b, norm1_g, norm2_g, w_in, ssd_conv_w, ssd_conv_b, ssd_dt_bias, ssd_a_log, ssd_d, ssd_norm_g, lru_conv_w, lru_conv_b, lru_w_r, lru_b_r, lru_w_i, lru_b_i, lru_lambda, w_br_attn, w_br_ssd, w_br_lru, w_out, router_wg, router_bg, router_we, router_be, exp_w_gate, exp_w_up, exp_w_down, final_g):
    depth = ada_w.shape[0]
    bsz, seq, d = x.shape
    n_tok = bsz * seq
    ssd_width = ssd_norm_g.shape[-1]
    xbc_width = ssd_conv_w.shape[-1]
    ssd_heads = ssd_dt_bias.shape[-1]
    lru_width = lru_lambda.shape[-1]
    o_z = 3 * ATTN_WIDTH
    o_xbc = o_z + ssd_width
    o_dt = o_xbc + xbc_width
    o_lru = o_dt + ssd_heads
    o_merge = o_lru + 2 * lru_width
    tn_big, tn_qkv, tn_lru = 1024, ATTN_WIDTH, lru_width
    c_xbc, c_z = 0, xbc_width
    c_merge = c_z + ssd_width
    c_qkv = c_merge + 3 * d
    c_lru = -(-(c_qkv + o_z) // tn_lru) * tn_lru
    c_dt = c_lru + 2 * lru_width
    seg = lambda a, b: w_in[:, :, a:b]
    zeros = lambda n: jnp.zeros(w_in.shape[:2] + (n,), w_in.dtype)
    w_all = jnp.concatenate([seg(o_xbc, o_dt), seg(o_z, o_xbc), seg(o_merge, w_in.shape[-1]),
                             rope_slab_layout(seg(0, 2 * ATTN_WIDTH)), seg(2 * ATTN_WIDTH, o_z),
                             zeros(c_lru - c_qkv - o_z), seg(o_lru, o_merge), seg(o_dt, o_lru),
                             zeros(LANES - ssd_heads)], axis=-1).astype(BF16)
    w_a, w_s, w_l, w_o = (w.astype(BF16) for w in (w_br_attn, w_br_ssd, w_br_lru, w_out))
    lru_w = lru_gate_weights(lru_w_r, lru_w_i)
    router_w, router_b = router_params(router_wg, router_bg, router_we, router_be)

    cond = ada_cond(c, ada_w, ada_b)
    cos, sin = rope_tables(positions)
    for l in range(depth):
        sh1, sc1, g1, sh2, sc2, g2 = [cond[l, :, i * d:(i + 1) * d] for i in range(6)]
        h = norm_modulate(x, norm1_g[l], sc1, sh1, BF16)
        h2d = h.reshape(n_tok, d)
        qkv = matmul_slabs(h, w_all, l, c_qkv, o_z, tn_qkv)
        z = matmul(h2d, w_all, l, c_z, ssd_width, 2048, tn_big, BF16)
        xbc = matmul(h2d, w_all, l, c_xbc, xbc_width, 2048, tn_big, BF16)
        dt_raw = matmul(h2d, w_all, l, c_dt, LANES, 2048, LANES)
        lru_gx = matmul(h2d, w_all, l, c_lru, 2 * lru_width, 2048, tn_lru, BF16)
        merge = matmul(h2d, w_all, l, c_merge, 3 * d, 2048, tn_big, BF16)
        y_attn = dilated_attention(qkv, cos, sin)
        y_ssd = ssd_mixer(xbc.reshape(bsz, seq, -1), dt_raw.reshape(bsz, seq, -1), z.reshape(bsz, seq, -1),
                          ssd_conv_w[l], ssd_conv_b[l], ssd_dt_bias[l], ssd_a_log[l], ssd_d[l], ssd_norm_g[l])
        y_lru = rglru_mixer(lru_gx.reshape(bsz, seq, -1), lru_conv_w[l], lru_conv_b[l], lru_w, lru_b_r[l],
                            lru_b_i[l], lru_lambda[l], l)
        x = merge_block(x, y_attn, y_ssd, y_lru, merge.reshape(bsz, seq, -1), g1, w_a, w_s, w_l, w_o, l)
        x = hier_moe_block(x, norm2_g[l], sc2, sh2, g2, router_w, router_b, exp_w_gate, exp_w_up, exp_w_down, l)
    return final_norm(x, final_g)
```
